```python
import jax
import jax.numpy as jnp
from jax import lax
import numpy as np

D_MODEL = 1024
BATCH = 4
SEQ = 4096
DEPTH = 2
DEC_BATCH = 128
DEC_SEQ = 1
PAST_LEN = 2048
PAGE_SIZE = 128

HEAD_DIM = 64
H_HGRN = 5
HGRN_DK = 64
HGRN_DV = 64
H_MLSTM = 5
MLSTM_DK = 64
MLSTM_DV = 64
SWA_PAIRS = ((128, 1), (512, 4), (2048, 16))
N_SWA_GROUPS = 3
SWA_HEADS_PER_GROUP = 2
H_SWA = N_SWA_GROUPS * SWA_HEADS_PER_GROUP
SWA_BLOCK = 128
CHUNK = 64
CONV_W = 4
W_HG_K = H_HGRN * HGRN_DK
W_HG_V = H_HGRN * HGRN_DV
W_ML = H_MLSTM * MLSTM_DV
W_SW = H_SWA * HEAD_DIM
MIX_WIDTH = W_HG_V + W_ML + W_SW
IN_SPLITS = (W_HG_K, W_HG_K, W_HG_V, W_HG_V, W_ML, W_ML, W_ML, H_MLSTM, H_MLSTM, W_SW, W_SW, W_SW)
N_IN = 2 * W_HG_K + 2 * W_HG_V + 3 * W_ML + 2 * H_MLSTM + 3 * W_SW
D_FF = 2816
N_EXPERTS = 8
TOP_K = 2
EXPERT_FF = 1408
N_DENSE = (DEPTH + 1) // 2
N_MOE = DEPTH // 2
EPS = 1e-6
F32 = jnp.float32

kernel_name = 'hybrid_hgrn2_mlstm_dilated_swa_decode_step'


def rmsnorm(x, g):
    xf = x.astype(F32)
    y = xf * lax.rsqrt(jnp.mean(xf * xf, axis=-1, keepdims=True) + EPS)
    return (y * g.astype(F32)).astype(x.dtype)


def head_rmsnorm(x, g):
    B, T, H, Dh = x.shape
    y = x * lax.rsqrt(jnp.mean(x * x, axis=-1, keepdims=True) + EPS)
    return y.reshape(B, T, H * Dh) * g.astype(F32)


def chunk_scan(fn, init, xs):
    B, T = xs[0].shape[:2]
    n = T // CHUNK
    xs_c = tuple(a.reshape((B, n, CHUNK) + a.shape[2:]).swapaxes(0, 1) for a in xs)
    final, ys = lax.scan(lambda c, a: fn(c, *a), init, xs_c)
    ys = ys.swapaxes(0, 1)
    return final, ys.reshape((B, T) + ys.shape[3:])


def hgrn_chunk(S, q, k, v, logf):
    S = S.astype(F32)
    L = q.shape[1]
    b = jnp.cumsum(logf, axis=1)
    causal = jnp.tril(jnp.ones((L, L), dtype=bool))[None, :, :, None, None]
    decay = jnp.exp(jnp.where(causal, b[:, :, None] - b[:, None, :], -jnp.inf))
    attn = jnp.sum(decay * q[:, :, None] * k[:, None, :], axis=-1)
    o = jnp.einsum('btsh,bshv->bthv', attn, v) + jnp.einsum('bthk,bhkv->bthv', q * jnp.exp(b), S)
    b_last = b[:, -1]
    S_new = jnp.exp(b_last)[..., None] * S + jnp.einsum('bshk,bshv->bhkv', k * jnp.exp(b_last[:, None] - b), v)
    return S_new, o


def mlstm_chunk(state, q, k, v, ig, lf):
    C, n, m = (a.astype(F32) for a in state)
    L = q.shape[1]
    F = jnp.cumsum(lf, axis=1)
    causal = jnp.tril(jnp.ones((L, L), dtype=bool))[None, :, :, None]
    logD = jnp.where(causal, F[:, :, None] - F[:, None, :] + ig[:, None, :], -jnp.inf)
    log_inter = F + m[:, None]
    m_t = jnp.maximum(log_inter, jnp.max(logD, axis=2))
    Dw = jnp.exp(logD - m_t[:, :, None])
    a_inter = jnp.exp(log_inter - m_t)
    qk = jnp.einsum('bthk,bshk->btsh', q, k) * Dw
    num = jnp.einsum('btsh,bshv->bthv', qk, v) + a_inter[..., None] * jnp.einsum('bthk,bhkv->bthv', q, C)
    nq = jnp.sum(qk, axis=2) + a_inter * jnp.einsum('bthk,bhk->bth', q, n)
    h = num / jnp.maximum(jnp.abs(nq), jnp.exp(-m_t))[..., None]
    m_new = m_t[:, -1]
    w = jnp.exp(F[:, -1:] - F + ig - m_new[:, None])
    a_c = jnp.exp(F[:, -1] + m - m_new)
    kw = k * w[..., None]
    C_new = a_c[..., None, None] * C + jnp.einsum('bshk,bshv->bhkv', kw, v)
    n_new = a_c[..., None] * n + jnp.sum(kw, axis=1)
    return (C_new, n_new, m_new), h


def causal_conv(u, buf, w, bias):
    T = u.shape[1]
    padded = jnp.concatenate([buf.astype(u.dtype), u], axis=1)
    y = bias.astype(F32)
    for j in range(CONV_W):
        y = y + w[j].astype(F32) * padded[:, j:j + T].astype(F32)
    return y, padded[:, T:]


def swa_prompt(q, k, v, window, dil):
    B, S, Hg, Dh = q.shape
    Ls = S // dil
    J = window // dil
    nblk = -(-Ls // SWA_BLOCK)
    pad = nblk * SWA_BLOCK - Ls

    def to_blocks(a):
        a = a.reshape(B, Ls, dil, Hg, Dh).transpose(0, 2, 1, 3, 4).astype(F32)
        a = jnp.pad(a, ((0, 0), (0, 0), (0, pad), (0, 0), (0, 0)))
        return a.reshape(B, dil, nblk, SWA_BLOCK, Hg, Dh)

    def with_prev(a):
        prev = jnp.pad(a, ((0, 0), (0, 0), (1, 0), (0, 0), (0, 0), (0, 0)))[:, :, :-1]
        return jnp.concatenate([prev, a], axis=3)

    def from_blocks(a):
        a = a.reshape((B, dil, nblk * SWA_BLOCK) + a.shape[4:])[:, :, :Ls]
        a = jnp.moveaxis(a, 1, 2)
        return a.reshape((B, S) + a.shape[3:])

    qb = to_blocks(q)
    kk = with_prev(to_blocks(k))
    vv = with_prev(to_blocks(v))
    s = jnp.einsum('brnqhd,brnkhd->brnhqk', qb, kk) * (Dh ** -0.5)
    qi = jnp.arange(SWA_BLOCK)[:, None]
    kj = jnp.arange(2 * SWA_BLOCK)[None, :]
    dist = qi + SWA_BLOCK - kj
    blk = jnp.arange(nblk)[:, None, None]
    mask = (dist >= 0) & (dist <= J) & (blk * SWA_BLOCK + kj - SWA_BLOCK >= 0)
    s = jnp.where(mask[None, None, :, None], s, -jnp.inf)
    mx = jnp.max(s, axis=-1, keepdims=True)
    p = jnp.exp(s - mx)
    den = jnp.sum(p, axis=-1)
    o = jnp.einsum('brnhqk,brnkhd->brnqhd', p, vv) / den.swapaxes(-1, -2)[..., None]
    lse = (mx[..., 0] + jnp.log(den)).swapaxes(-1, -2)
    return from_blocks(o), from_blocks(lse)


def swa_sample(q, k, v, buf, window, dil):
    N, T, Hg, Dh = q.shape
    Wb = buf.shape[1]
    J = window // dil
    kk = jnp.concatenate([buf[:, :, 0].astype(F32), k.astype(F32)], axis=1)
    vv = jnp.concatenate([buf[:, :, 1].astype(F32), v.astype(F32)], axis=1)
    idx = Wb + jnp.arange(T)[:, None] - dil * jnp.arange(J + 1)[None, :]
    valid = idx >= 0
    idx = jnp.maximum(idx, 0)
    kg = kk[:, idx]
    vg = vv[:, idx]
    s = jnp.einsum('nthd,ntjhd->nthj', q.astype(F32), kg) * (Dh ** -0.5)
    s = jnp.where(valid[None, :, None, :], s, -jnp.inf)
    mx = jnp.max(s, axis=-1, keepdims=True)
    p = jnp.exp(s - mx)
    den = jnp.sum(p, axis=-1)
    o = jnp.einsum('nthj,ntjhd->nthd', p, vg) / den[..., None]
    lse = mx[..., 0] + jnp.log(den)
    new_buf = jnp.concatenate([buf, jnp.stack([k, v], axis=2).astype(buf.dtype)], axis=1)[:, T:]
    return o, lse, new_buf


def mixer(h, lp, st):
    B, T, _ = h.shape
    proj = h @ lp['w_in']
    (hq, hf, hi, hg, mu, mv, mo, mig, mfg, sq, sk, sv) = jnp.split(
        proj, np.cumsum(IN_SPLITS)[:-1].tolist(), axis=-1)

    lb = lp['lb']
    z = hf.astype(F32)
    logf_h = jnp.logaddexp(jnp.log(lb), jnp.log1p(-lb) + jax.nn.log_sigmoid(z))
    k_h = (1.0 - lb) * jax.nn.sigmoid(-z)
    q_h = jax.nn.silu(hq.astype(F32))
    shp_k = (B, T, H_HGRN, HGRN_DK)
    args_h = (q_h.reshape(shp_k), k_h.reshape(shp_k),
              hi.astype(F32).reshape(B, T, H_HGRN, HGRN_DV), logf_h.reshape(shp_k))
    if st is None:
        S_h, o_h = chunk_scan(hgrn_chunk, jnp.zeros((B, H_HGRN, HGRN_DK, HGRN_DV), F32), args_h)
    else:
        S_h, o_h = hgrn_chunk(st['hgrn'], *args_h)
    out_h = head_rmsnorm(o_h, lp['hgrn_g']) * jax.nn.silu(hg.astype(F32))

    buf = jnp.zeros((B, CONV_W - 1, W_ML), mu.dtype) if st is None else st['conv']
    cu, conv_new = causal_conv(mu, buf, lp['conv_w'], lp['conv_b'])
    c = jax.nn.silu(cu).reshape(B, T, H_MLSTM, MLSTM_DV)
    q_m = jnp.einsum('bthc,hcd->bthd', c, lp['ml_wq'].astype(F32))
    k_m = jnp.einsum('bthc,hcd->bthd', c, lp['ml_wk'].astype(F32)) * (MLSTM_DK ** -0.5)
    v_m = mv.astype(F32).reshape(B, T, H_MLSTM, MLSTM_DV)
    ig = mig.astype(F32) + lp['ig_b'].astype(F32)
    lf = jax.nn.log_sigmoid(mfg.astype(F32) + lp['fg_b'].astype(F32))
    if st is None:
        init = (jnp.zeros((B, H_MLSTM, MLSTM_DK, MLSTM_DV), F32),
                jnp.zeros((B, H_MLSTM, MLSTM_DK), F32), jnp.zeros((B, H_MLSTM), F32))
        (C_m, n_m, m_m), o_m = chunk_scan(mlstm_chunk, init, (q_m, k_m, v_m, ig, lf))
    else:
        (C_m, n_m, m_m), o_m = mlstm_chunk((st['C'], st['n'], st['m']), q_m, k_m, v_m, ig, lf)
    out_m = head_rmsnorm(o_m, lp['ml_g']) * jax.nn.sigmoid(mo.astype(F32))

    q_s = sq.reshape(B, T, H_SWA, HEAD_DIM)
    k_s = sk.reshape(B, T, H_SWA, HEAD_DIM)
    v_s = sv.reshape(B, T, H_SWA, HEAD_DIM)
    outs, lses, bufs = [], [], []
    for g, (win, dil) in enumerate(SWA_PAIRS):
        hs = slice(g * SWA_HEADS_PER_GROUP, (g + 1) * SWA_HEADS_PER_GROUP)
        if st is None:
            o, lse = swa_prompt(q_s[:, :, hs], k_s[:, :, hs], v_s[:, :, hs], win, dil)
            keep = min(win, T)
            nb = jnp.stack([k_s[:, T - keep:, hs], v_s[:, T - keep:, hs]], axis=2)
        else:
            o, lse, nb = swa_sample(q_s[:, :, hs], k_s[:, :, hs], v_s[:, :, hs], st['swa'][g], win, dil)
        outs.append(o)
        lses.append(lse)
        bufs.append(nb)
    alpha = jax.nn.softmax(jnp.stack(lses, axis=2), axis=2)
    out_s = (jnp.stack(outs, axis=2) * alpha[..., None]).reshape(B, T, W_SW)

    mix = jnp.concatenate([out_h, out_m, out_s], axis=-1).astype(h.dtype)
    new_st = {'hgrn': S_h, 'C': C_m, 'n': n_m, 'm': m_m, 'conv': conv_new, 'swa': bufs}
    return mix, new_st


def swiglu(x, wg, wu, wd):
    return (jax.nn.silu(x @ wg) * (x @ wu)) @ wd


def moe_ffn(x, router_w, router_b, wg, wu, wd):
    B, T, D = x.shape
    xf = x.reshape(B * T, D)
    logits = (xf @ router_w).astype(F32) + router_b.astype(F32)
    top_v, top_i = lax.top_k(logits, TOP_K)
    w_k = jax.nn.softmax(top_v, axis=-1)
    gates = jnp.sum(jax.nn.one_hot(top_i, N_EXPERTS, dtype=F32) * w_k[..., None], axis=1)
    y = jnp.zeros((B * T, D), F32)
    for e in range(N_EXPERTS):
        y = y + gates[:, e:e + 1] * swiglu(xf, wg[e], wu[e], wd[e]).astype(F32)
    return y.astype(x.dtype).reshape(B, T, D)


def block(x, lp, st, ffn_fn, ffn_p):
    mix, new_st = mixer(rmsnorm(x, lp['n1']), lp, st)
    x = x + mix @ lp['w_out']
    x = x + ffn_fn(rmsnorm(x, lp['n2']), *ffn_p)
    return x, new_st


def setup_inputs(seed: int = 0) -> dict:
    key = jax.random.key(seed)
    keys = iter(jax.random.split(key, 48))

    def nrm(shape, scale):
        return scale * jax.random.normal(next(keys), shape, F32)

    win = [min(w, PAST_LEN) for w, _ in SWA_PAIRS]
    return {
        'x_prompt': nrm((BATCH, SEQ, D_MODEL), 1.0),
        'x_sample': nrm((DEC_BATCH, DEC_SEQ, D_MODEL), 1.0),
        'state_hgrn': nrm((DEPTH, DEC_BATCH, H_HGRN, HGRN_DK, HGRN_DV), 0.3),
        'state_mlstm_C': nrm((DEPTH, DEC_BATCH, H_MLSTM, MLSTM_DK, MLSTM_DV), 0.3),
        'state_mlstm_n': nrm((DEPTH, DEC_BATCH, H_MLSTM, MLSTM_DK), 0.3),
        'state_mlstm_m': nrm((DEPTH, DEC_BATCH, H_MLSTM), 1.0),
        'state_mlstm_conv': nrm((DEPTH, DEC_BATCH, CONV_W - 1, W_ML), 1.0),
        'cache_swa_w128': nrm((DEPTH, DEC_BATCH, win[0], 2, SWA_HEADS_PER_GROUP, HEAD_DIM), 1.0),
        'cache_swa_w512': nrm((DEPTH, DEC_BATCH, win[1], 2, SWA_HEADS_PER_GROUP, HEAD_DIM), 1.0),
        'cache_swa_w2048': nrm((DEPTH, DEC_BATCH, win[2], 2, SWA_HEADS_PER_GROUP, HEAD_DIM), 1.0),
        'norm1_g': 1.0 + nrm((DEPTH, D_MODEL), 0.1),
        'norm2_g': 1.0 + nrm((DEPTH, D_MODEL), 0.1),
        'final_norm_g': 1.0 + nrm((D_MODEL,), 0.1),
        'w_in': nrm((DEPTH, D_MODEL, N_IN), D_MODEL ** -0.5),
        'w_out': nrm((DEPTH, MIX_WIDTH, D_MODEL), MIX_WIDTH ** -0.5),
        'hgrn_lb_param': nrm((DEPTH, W_HG_K), 0.5),
        'hgrn_onorm_g': 1.0 + nrm((DEPTH, W_HG_V), 0.1),
        'mlstm_conv_w': nrm((DEPTH, CONV_W, W_ML), CONV_W ** -0.5),
        'mlstm_conv_b': nrm((DEPTH, W_ML), 0.01),
        'mlstm_wq': nrm((DEPTH, H_MLSTM, MLSTM_DV, MLSTM_DK), MLSTM_DV ** -0.5),
        'mlstm_wk': nrm((DEPTH, H_MLSTM, MLSTM_DV, MLSTM_DK), MLSTM_DV ** -0.5),
        'mlstm_ig_b': nrm((DEPTH, H_MLSTM), 0.1),
        'mlstm_fg_b': jnp.linspace(3.0, 6.0, H_MLSTM, dtype=F32)[None] + nrm((DEPTH, H_MLSTM), 0.1),
        'mlstm_onorm_g': 1.0 + nrm((DEPTH, W_ML), 0.1),
        'ffn_w_gate': nrm((N_DENSE, D_MODEL, D_FF), D_MODEL ** -0.5),
        'ffn_w_up': nrm((N_DENSE, D_MODEL, D_FF), D_MODEL ** -0.5),
        'ffn_w_down': nrm((N_DENSE, D_FF, D_MODEL), D_FF ** -0.5),
        'moe_router_w': nrm((N_MOE, D_MODEL, N_EXPERTS), D_MODEL ** -0.5),
        'moe_router_b': nrm((N_MOE, N_EXPERTS), 0.01),
        'moe_w_gate': nrm((N_MOE, N_EXPERTS, D_MODEL, EXPERT_FF), D_MODEL ** -0.5),
        'moe_w_up': nrm((N_MOE, N_EXPERTS, D_MODEL, EXPERT_FF), D_MODEL ** -0.5),
        'moe_w_down': nrm((N_MOE, N_EXPERTS, EXPERT_FF, D_MODEL), EXPERT_FF ** -0.5),
    }


def reference(x_prompt, x_sample, state_hgrn, state_mlstm_C, state_mlstm_n, state_mlstm_m,
              state_mlstm_conv, cache_swa_w128, cache_swa_w512, cache_swa_w2048,
              norm1_g, norm2_g, final_norm_g, w_in, w_out, hgrn_lb_param, hgrn_onorm_g,
              mlstm_conv_w, mlstm_conv_b, mlstm_wq, mlstm_wk, mlstm_ig_b, mlstm_fg_b, mlstm_onorm_g,
              ffn_w_gate, ffn_w_up, ffn_w_down, moe_router_w, moe_router_b,
              moe_w_gate, moe_w_up, moe_w_down):
    p_lb = jax.nn.softmax(hgrn_lb_param.astype(F32), axis=0)
    cs = jnp.cumsum(p_lb, axis=0)
    lb_all = cs - cs[0:1]

    xp, xs = x_prompt, x_sample
    sp_list, ss_list = [], []
    for li in range(DEPTH):
        lp = {'n1': norm1_g[li], 'n2': norm2_g[li], 'w_in': w_in[li], 'w_out': w_out[li],
              'lb': lb_all[li], 'hgrn_g': hgrn_onorm_g[li], 'conv_w': mlstm_conv_w[li],
              'conv_b': mlstm_conv_b[li], 'ml_wq': mlstm_wq[li], 'ml_wk': mlstm_wk[li],
              'ig_b': mlstm_ig_b[li], 'fg_b': mlstm_fg_b[li], 'ml_g': mlstm_onorm_g[li]}
        j = li // 2
        if li % 2 == 0:
            ffn_fn, ffn_p = swiglu, (ffn_w_gate[j], ffn_w_up[j], ffn_w_down[j])
        else:
            ffn_fn, ffn_p = moe_ffn, (moe_router_w[j], moe_router_b[j], moe_w_gate[j], moe_w_up[j], moe_w_down[j])
        st = {'hgrn': state_hgrn[li], 'C': state_mlstm_C[li], 'n': state_mlstm_n[li],
              'm': state_mlstm_m[li], 'conv': state_mlstm_conv[li],
              'swa': [cache_swa_w128[li], cache_swa_w512[li], cache_swa_w2048[li]]}
        xp, sp = block(xp, lp, None, ffn_fn, ffn_p)
        xs, ss = block(xs, lp, st, ffn_fn, ffn_p)
        sp_list.append(sp)
        ss_list.append(ss)

    y_prompt = rmsnorm(xp, final_norm_g)
    y_sample = rmsnorm(xs, final_norm_g)

    def stk(lst, name):
        return jnp.stack([s[name] for s in lst], axis=0)

    def stk_swa(lst, g):
        return jnp.stack([s['swa'][g] for s in lst], axis=0)

    return (y_prompt, y_sample,
            stk(sp_list, 'hgrn'), stk(ss_list, 'hgrn'),
            stk(sp_list, 'C'), stk(ss_list, 'C'),
            stk(sp_list, 'n'), stk(ss_list, 'n'),
            stk(sp_list, 'm'), stk(ss_list, 'm'),
            stk(sp_list, 'conv'), stk(ss_list, 'conv'),
            stk_swa(sp_list, 0), stk_swa(ss_list, 0),
            stk_swa(sp_list, 1), stk_swa(ss_list, 1),
            stk_swa(sp_list, 2), stk_swa(ss_list, 2))
```

```python
import functools

import numpy as np
import jax
import jax.numpy as jnp
from jax import lax
from jax.experimental import pallas as pl
from jax.experimental.pallas import tpu as pltpu

F32 = jnp.float32
BF16 = jnp.bfloat16

D_MODEL = 1024
HEAD = 64
N_HEADS = 5
N_PAIRS = 3
SEG = 384
LANES = 128
FLAT = HEAD * HEAD
CONV_W = 4
SWA_PAIRS = ((128, 1), (512, 4), (2048, 16))
SWA_J = 128
N_GROUPS = 3
EPS = 1e-6
N_EXPERTS = 8
IN_SPLITS = (320, 320, 320, 320, 320, 320, 320, 5, 5, 384, 384, 384)

OFF_HQ, OFF_HF, OFF_HI, OFF_HG = 0, 384, 768, 1152
OFF_MU, OFF_MV, OFF_MO = 1536, 1920, 2304
OFF_SQ, OFF_SK, OFF_SV = 2688, 3072, 3456
OFF_GATE = 3840
N_PROJ = 3968

CHUNK = 128
VMEM_LIMIT = 56 * 1024 * 1024


def _cparams(*sem):
    return pltpu.CompilerParams(dimension_semantics=sem, vmem_limit_bytes=VMEM_LIMIT)


def _dot(a, b):
    return jnp.dot(a, b, preferred_element_type=F32)


def _dot_nt(a, b):
    return lax.dot_general(a, b, (((1,), (1,)), ((), ())), preferred_element_type=F32)


def _split(x, n):
    parts = []
    r = x
    for i in range(n):
        p = r.astype(BF16)
        parts.append(p)
        if i + 1 < n:
            r = r - p.astype(F32)
    return parts


def _dot_sel(x, sel, n=3):
    acc = None
    for p in _split(x, n):
        t = _dot(p, sel)
        acc = t if acc is None else acc + t
    return acc


def _sel_dot(sel, x, n=3):
    acc = None
    for p in _split(x, n):
        t = _dot(sel, p)
        acc = t if acc is None else acc + t
    return acc


def _sel_dot_nt(sel, x, n=3):
    acc = None
    for p in _split(x, n):
        t = _dot_nt(sel, p)
        acc = t if acc is None else acc + t
    return acc


def _sigmoid(x):
    return 1.0 / (1.0 + jnp.exp(-x))


def _log_sigmoid(x):
    return jnp.minimum(x, 0.0) - jnp.log1p(jnp.exp(-jnp.abs(x)))


def _iota(shape, axis):
    return lax.broadcasted_iota(jnp.int32, shape, axis)


def _tril_bf16(n):
    return jnp.where(_iota((n, n), 0) >= _iota((n, n), 1), 1.0, 0.0).astype(BF16)


def _same_head(n_rows, n_cols):
    return (_iota((n_rows, n_cols), 0) // HEAD) == (_iota((n_rows, n_cols), 1) // HEAD)


def _head_ones():
    return jnp.where(_same_head(LANES, LANES), 1.0, 0.0).astype(BF16)


def _stack_heads(x):
    lane = _iota(x.shape, 1)
    zero = jnp.zeros_like(x)
    return jnp.concatenate([jnp.where(lane < HEAD, x, zero).astype(BF16),
                            jnp.where(lane >= HEAD, x, zero).astype(BF16)], axis=0)


def _stack_heads_ones(x):
    lane = _iota(x.shape, 1)
    zero = jnp.zeros_like(x)
    one = jnp.ones_like(x)
    top = jnp.concatenate([jnp.where(lane < HEAD, x, zero), jnp.where(lane < HEAD, one, zero)], axis=1)
    bot = jnp.concatenate([jnp.where(lane >= HEAD, x, zero), jnp.where(lane >= HEAD, one, zero)], axis=1)
    return jnp.concatenate([top.astype(BF16), bot.astype(BF16)], axis=0)


def _pair_sums(x, n=2):
    ones = _head_ones()
    cols = [_dot_sel(x[:, p * LANES:(p + 1) * LANES], ones, n) for p in range(x.shape[1] // LANES)]
    return cols[0] if len(cols) == 1 else jnp.concatenate(cols, axis=1)


def _head_rms(o, g):
    msq = _pair_sums(o * o) * (1.0 / HEAD)
    return o * lax.rsqrt(msq + EPS) * g


def _rmsnorm(x, g):
    return x * lax.rsqrt(jnp.mean(x * x, axis=-1, keepdims=True) + EPS) * g


def _norm_proj_kernel(x_ref, g_ref, w_ref, o_ref):
    o_ref[...] = _dot(_rmsnorm(x_ref[...], g_ref[...]).astype(BF16), w_ref[...])


def norm_proj(x, g, w, tm):
    m = x.shape[0]
    n = w.shape[1]
    return pl.pallas_call(
        _norm_proj_kernel,
        grid=(m // tm,),
        in_specs=[pl.BlockSpec((tm, D_MODEL), lambda i: (i, 0)),
                  pl.BlockSpec((1, D_MODEL), lambda i: (0, 0)),
                  pl.BlockSpec((D_MODEL, n), lambda i: (0, 0))],
        out_specs=pl.BlockSpec((tm, n), lambda i: (i, 0)),
        out_shape=jax.ShapeDtypeStruct((m, n), F32),
        compiler_params=_cparams("parallel"),
    )(x, g, w)


def _hgrn_level_map(L):
    t = np.arange(L)[:, None]
    s = np.arange(L)[None, :]
    lev = np.full((L, L), -1, np.int32)
    n_lev = int(np.log2(L))
    for k in range(n_lev):
        h = L >> (k + 1)
        ok = (t // (2 * h) == s // (2 * h)) & (t % (2 * h) >= h) & (s % (2 * h) < h)
        lev[ok] = k
    lev[t == s] = n_lev
    return np.concatenate([lev, lev], axis=0)


def _block_ref(b, h):
    L = b.shape[0]
    if 2 * h >= 8:
        pieces = []
        for j in range(L // (2 * h)):
            r0 = j * 2 * h + h - 1
            pieces.append(jnp.broadcast_to(b[r0:r0 + 1, :], (2 * h, b.shape[1])))
        return jnp.concatenate(pieces, axis=0)
    pos = _iota(b.shape, 0) % (2 * h)
    out = b
    for d in range(-(h - 1), h + 1):
        if d != 0:
            out = jnp.where(pos == h - 1 + d, pltpu.roll(b, d % L, axis=0), out)
    return out


def _hgrn_gates(hq, z, lb):
    la = jnp.log(lb)
    lc = jnp.log1p(-lb) + _log_sigmoid(z)
    logf = jnp.maximum(la, lc) + jnp.log1p(jnp.exp(-jnp.abs(la - lc)))
    kk = (1.0 - lb) * _sigmoid(-z)
    q = hq * _sigmoid(hq)
    return logf, kk, q


def _hgrn_kernel(p_ref, lb_ref, g_ref, lev_ref, o_ref, s_out_ref, s_ref):
    c = pl.program_id(1)
    L = p_ref.shape[0]

    @pl.when(c == 0)
    def _():
        s_ref[...] = jnp.zeros_like(s_ref)

    hq = p_ref[:, OFF_HQ:OFF_HQ + SEG]
    z = p_ref[:, OFF_HF:OFF_HF + SEG]
    v = p_ref[:, OFF_HI:OFF_HI + SEG]
    hg = p_ref[:, OFF_HG:OFF_HG + SEG]
    logf, kk, q = _hgrn_gates(hq, z, lb_ref[...])

    b = _sel_dot(_tril_bf16(L), logf)
    b_last = b[L - 1:L, :]

    lev = lev_ref[...]
    n_lev = int(np.log2(L))
    row = _iota((L, SEG), 0)
    acc = [jnp.zeros((2 * L, L), F32) for _ in range(N_PAIRS)]
    for k in range(n_lev + 1):
        if k < n_lev:
            h = L >> (k + 1)
            ex = jnp.exp(-jnp.abs(b - _block_ref(b, h)))
            xl = jnp.where(row % (2 * h) >= h, q, kk) * ex
            xr = xl
        else:
            xl = q
            xr = kk
        for p in range(N_PAIRS):
            sl = slice(p * LANES, (p + 1) * LANES)
            sc = _dot_nt(_stack_heads(xl[:, sl]), xr[:, sl].astype(BF16))
            acc[p] = jnp.where(lev == k, sc, acc[p])

    qd = (q * jnp.exp(b)).astype(BF16)
    kd = kk * jnp.exp(b_last - b)
    vb = v.astype(BF16)
    same = _same_head(LANES, LANES)
    outs = []
    for p in range(N_PAIRS):
        sl = slice(p * LANES, (p + 1) * LANES)
        a2 = jnp.concatenate([acc[p][:L], acc[p][L:]], axis=1).astype(BF16)
        s_old = s_ref[p]
        outs.append(_dot(a2, _stack_heads(v[:, sl])) + _dot(qd[:, sl], s_old.astype(BF16)))
        dec = jnp.broadcast_to(jnp.exp(b_last[:, sl]), (LANES, LANES)).T
        kv = _dot(kd[:, sl].T.astype(BF16), vb[:, sl])
        s_ref[p] = dec * s_old + jnp.where(same, kv, 0.0)
    o = jnp.concatenate(outs, axis=1)
    o_ref[...] = _head_rms(o, g_ref[...]) * (hg * _sigmoid(hg))

    @pl.when(c == pl.num_programs(1) - 1)
    def _():
        s_out_ref[...] = s_ref[...]


def hgrn_prompt(proj, lb, g, batch, seq):
    L = CHUNK
    nc = seq // L
    lev = jnp.asarray(_hgrn_level_map(L))
    return pl.pallas_call(
        _hgrn_kernel,
        grid=(batch, nc),
        in_specs=[pl.BlockSpec((L, 4 * SEG), lambda b, c: (b * nc + c, 0)),
                  pl.BlockSpec((1, SEG), lambda b, c: (0, 0)),
                  pl.BlockSpec((1, SEG), lambda b, c: (0, 0)),
                  pl.BlockSpec((2 * L, L), lambda b, c: (0, 0))],
        out_specs=[pl.BlockSpec((L, SEG), lambda b, c: (b * nc + c, 0)),
                   pl.BlockSpec((None, N_PAIRS, LANES, LANES), lambda b, c: (b, 0, 0, 0))],
        out_shape=[jax.ShapeDtypeStruct((batch * seq, SEG), F32),
                   jax.ShapeDtypeStruct((batch, N_PAIRS, LANES, LANES), F32)],
        scratch_shapes=[pltpu.VMEM((N_PAIRS, LANES, LANES), F32)],
        compiler_params=_cparams("parallel", "arbitrary"),
    )(proj, lb, g, lev)


def _gate_selectors():
    x_ig = np.zeros((LANES, SEG), np.float32)
    x_f = np.zeros((LANES, SEG), np.float32)
    x_ft = np.zeros((LANES, N_HEADS * LANES), np.float32)
    sel_d = np.zeros((8, LANES), np.float32)
    for h in range(N_HEADS):
        x_ig[h, h * HEAD:(h + 1) * HEAD] = 1.0
        x_f[N_HEADS + h, h * HEAD:(h + 1) * HEAD] = 1.0
        x_ft[N_HEADS + h, h * LANES:(h + 1) * LANES] = 1.0
        sel_d[h, h] = 1.0
        sel_d[h, N_HEADS + h] = -1.0
    return (jnp.asarray(x_ig, BF16), jnp.asarray(x_f, BF16), jnp.asarray(x_ft, BF16),
            jnp.asarray(sel_d, BF16))


def _mlstm_kernel(mu_ref, mv_ref, mo_ref, gate_ref, cw_ref, cb_ref, wq_ref, wk_ref, gb_ref, g_ref,
                  xig_ref, xf_ref, xft_ref, seld_ref,
                  o_ref, c_out_ref, n_out_ref, m_out_ref,
                  u_ref, c_ref, n_ref, m1_ref, m2_ref):
    ci = pl.program_id(1)
    L = mu_ref.shape[0]

    @pl.when(ci == 0)
    def _():
        u_ref[0:8, :] = jnp.zeros((8, SEG), F32)
        c_ref[...] = jnp.zeros_like(c_ref)
        n_ref[...] = jnp.zeros_like(n_ref)
        m1_ref[...] = jnp.zeros_like(m1_ref)
        m2_ref[...] = jnp.zeros_like(m2_ref)

    @pl.when(ci > 0)
    def _():
        u_ref[0:8, :] = u_ref[L:L + 8, :]

    u_ref[8:L + 8, :] = mu_ref[...]

    y = cb_ref[...] + cw_ref[CONV_W - 1:CONV_W, :] * u_ref[8:L + 8, :]
    for j in range(1, CONV_W):
        y = y + cw_ref[CONV_W - 1 - j:CONV_W - j, :] * u_ref[8 - j:8 - j + L, :]
    cact = (y * _sigmoid(y)).astype(BF16)
    v = mv_ref[...]
    vb = v.astype(BF16)

    gfull = gate_ref[...] + gb_ref[...]
    lane = _iota((L, LANES), 1)
    fcum = _sel_dot(_tril_bf16(L), _log_sigmoid(gfull))
    gv = jnp.where(lane < N_HEADS, gfull, fcum)
    igx = _dot_sel(gv, xig_ref[...])
    fx = _dot_sel(gv, xf_ref[...])
    ftile = _dot_sel(gv, xft_ref[...])
    rowv = _sel_dot_nt(seld_ref[...], gv)
    m1 = m1_ref[...]
    m2 = m2_ref[...]

    causal = _iota((L, L), 0) >= _iota((L, L), 1)
    lane_p = _iota((L, LANES), 1)
    same = _same_head(LANES, LANES)
    ones_l = jnp.ones((L, LANES), BF16)

    dws, mtt = [], []
    for h in range(N_HEADS):
        ft = ftile[:, h * LANES:(h + 1) * LANES]
        logd = jnp.where(causal, ft + rowv[h:h + 1, :], -jnp.inf)
        mt = jnp.maximum(ft + m2[:, h * LANES:(h + 1) * LANES], jnp.max(logd, axis=-1, keepdims=True))
        dws.append(jnp.exp(logd - mt))
        mtt.append(mt)
    m2_ref[...] = jnp.concatenate([mt[L - 1:L, :] for mt in mtt], axis=1)
    zeros_t = jnp.zeros((L, LANES), F32)
    mtx = jnp.concatenate([jnp.where(lane_p < HEAD, mtt[0], mtt[1]),
                           jnp.where(lane_p < HEAD, mtt[2], mtt[3]),
                           jnp.where(lane_p < HEAD, mtt[4], zeros_t)], axis=1)
    a_int = jnp.exp(fx + m1 - mtx)
    m1_new = mtx[L - 1:L, :]
    w = jnp.exp(fx[L - 1:L, :] - fx + igx - m1_new)
    a_c = jnp.exp(fx[L - 1:L, :] + m1 - m1_new)
    m1_ref[...] = m1_new

    nums, nqs = [], []
    for p in range(N_PAIRS):
        sl = slice(p * LANES, (p + 1) * LANES)
        cp = cact[:, sl]
        q = _dot(cp, wq_ref[p])
        k = _dot(cp, wk_ref[p])
        sc = _dot_nt(_stack_heads(q), k.astype(BF16))
        p0 = sc[:L] * dws[2 * p]
        p1 = sc[L:] * dws[2 * p + 1] if 2 * p + 1 < N_HEADS else jnp.zeros((L, L), F32)
        a2 = jnp.concatenate([p0, p1], axis=1).astype(BF16)
        intra = _dot(a2, _stack_heads_ones(v[:, sl]))
        c_old = c_ref[p]
        n_old = n_ref[p]
        inter = _dot(q.astype(BF16), jnp.concatenate([c_old, n_old], axis=1).astype(BF16))
        ai = a_int[:, sl]
        nums.append(intra[:, :LANES] + ai * inter[:, :LANES])
        nqs.append(intra[:, LANES:] + ai * inter[:, LANES:])
        kw = k * w[:, sl]
        kw_hi, kw_lo = _split(kw.T, 2)
        upd = _dot(kw_hi, jnp.concatenate([vb[:, sl], ones_l], axis=1))
        nsum = upd[:, LANES:] + _dot(kw_lo, ones_l)
        acp = a_c[:, sl]
        c_ref[p] = acp * c_old + jnp.where(same, upd[:, :LANES], 0.0)
        n_ref[p] = acp * n_old + jnp.where(same, nsum, 0.0)
    num = jnp.concatenate(nums, axis=1)
    nq = jnp.concatenate(nqs, axis=1)
    hh = num / jnp.maximum(jnp.abs(nq), jnp.exp(-mtx))
    o_ref[...] = _head_rms(hh, g_ref[...]) * _sigmoid(mo_ref[...])

    @pl.when(ci == pl.num_programs(1) - 1)
    def _():
        c_out_ref[...] = c_ref[...]
        n_out_ref[...] = n_ref[...]
        m_out_ref[...] = jnp.broadcast_to(m1_ref[...], (8, SEG))


def mlstm_prompt(proj, cw, cb, wq, wk, gb, g, batch, seq):
    L = CHUNK
    nc = seq // L
    x_ig, x_f, x_ft, sel_d = _gate_selectors()
    row = lambda b, c: (b * nc + c, 0)
    const2 = lambda b, c: (0, 0)
    const3 = lambda b, c: (0, 0, 0)
    st_spec = pl.BlockSpec((None, N_PAIRS, LANES, LANES), lambda b, c: (b, 0, 0, 0))
    st_shape = jax.ShapeDtypeStruct((batch, N_PAIRS, LANES, LANES), F32)
    return pl.pallas_call(
        _mlstm_kernel,
        grid=(batch, nc),
        in_specs=[pl.BlockSpec((L, SEG), lambda b, c: (b * nc + c, OFF_MU // SEG)),
                  pl.BlockSpec((L, SEG), lambda b, c: (b * nc + c, OFF_MV // SEG)),
                  pl.BlockSpec((L, SEG), lambda b, c: (b * nc + c, OFF_MO // SEG)),
                  pl.BlockSpec((L, LANES), lambda b, c: (b * nc + c, OFF_GATE // LANES)),
                  pl.BlockSpec((8, SEG), const2),
                  pl.BlockSpec((1, SEG), const2),
                  pl.BlockSpec((N_PAIRS, LANES, LANES), const3),
                  pl.BlockSpec((N_PAIRS, LANES, LANES), const3),
                  pl.BlockSpec((1, LANES), const2),
                  pl.BlockSpec((1, SEG), const2),
                  pl.BlockSpec((LANES, SEG), const2),
                  pl.BlockSpec((LANES, SEG), const2),
                  pl.BlockSpec((LANES, N_HEADS * LANES), const2),
                  pl.BlockSpec((8, LANES), const2)],
        out_specs=[pl.BlockSpec((L, SEG), row), st_spec, st_spec,
                   pl.BlockSpec((None, 8, SEG), lambda b, c: (b, 0, 0))],
        out_shape=[jax.ShapeDtypeStruct((batch * seq, SEG), F32), st_shape, st_shape,
                   jax.ShapeDtypeStruct((batch, 8, SEG), F32)],
        scratch_shapes=[pltpu.VMEM((L + 8, SEG), F32),
                        pltpu.VMEM((N_PAIRS, LANES, LANES), F32),
                        pltpu.VMEM((N_PAIRS, LANES, LANES), F32),
                        pltpu.VMEM((1, SEG), F32),
                        pltpu.VMEM((1, N_HEADS * LANES), F32)],
        compiler_params=_cparams("parallel", "arbitrary"),
    )(proj, proj, proj, proj, cw, cb, wq, wk, gb, g, x_ig, x_f, x_ft, sel_d)


def _swa_kernel(q_ref, kp_ref, kc_ref, vp_ref, vc_ref, o_ref, lse_ref):
    blk = pl.program_id(2)
    L = q_ref.shape[0]
    q = q_ref[...] * (HEAD ** -0.5)
    kcat = jnp.concatenate([kp_ref[...], kc_ref[...]], axis=0).astype(BF16)
    vcat = jnp.concatenate([vp_ref[...], vc_ref[...]], axis=0)
    s = _dot_nt(_stack_heads(q), kcat)
    qi = _iota((2 * L, 2 * L), 0) % L
    kj = _iota((2 * L, 2 * L), 1)
    dist = qi + L - kj
    valid = (dist >= 0) & (dist <= SWA_J) & (blk * L + kj - L >= 0)
    s = jnp.where(valid, s, -jnp.inf)
    mx = jnp.max(s, axis=-1, keepdims=True)
    p = jnp.exp(s - mx)
    p2 = jnp.concatenate([p[:L], p[L:]], axis=1).astype(BF16)
    r = _dot(p2, _stack_heads_ones(vcat))
    den = r[:, LANES:]
    lane = _iota((L, LANES), 1)
    mxt = jnp.where(lane < HEAD, mx[:L], mx[L:])
    o_ref[...] = r[:, :LANES] / den
    lse_ref[...] = mxt + jnp.log(den)


def swa_prompt(q, k, v):
    batch, dil, ls, _ = q.shape
    L = SWA_J
    nb = ls // L
    cur = pl.BlockSpec((None, None, L, LANES), lambda b, r, i: (b, r, i, 0))
    prev = pl.BlockSpec((None, None, L, LANES), lambda b, r, i: (b, r, jnp.maximum(i - 1, 0), 0))
    shape = jax.ShapeDtypeStruct(q.shape, F32)
    return pl.pallas_call(
        _swa_kernel,
        grid=(batch, dil, nb),
        in_specs=[cur, prev, cur, prev, cur],
        out_specs=[cur, cur],
        out_shape=[shape, shape],
        compiler_params=_cparams("parallel", "parallel", "arbitrary"),
    )(q, k, k, v, v)


def _mix_out_kernel(oh_ref, om_ref, o0_ref, o1_ref, o2_ref, l0_ref, l1_ref, l2_ref, x_ref, w_ref, y_ref):
    l0, l1, l2 = l0_ref[...], l1_ref[...], l2_ref[...]
    mx = jnp.maximum(jnp.maximum(l0, l1), l2)
    e0, e1, e2 = jnp.exp(l0 - mx), jnp.exp(l1 - mx), jnp.exp(l2 - mx)
    inv = 1.0 / (e0 + e1 + e2)
    mix = jnp.concatenate([oh_ref[...], om_ref[...], o0_ref[...] * (e0 * inv), o1_ref[...] * (e1 * inv),
                           o2_ref[...] * (e2 * inv)], axis=1)
    y_ref[...] = x_ref[...] + _dot(mix.astype(BF16), w_ref[...])


def mix_out(oh, om, os_, ls_, x, w, tm):
    m = x.shape[0]
    row = lambda i: (i, 0)
    seg = pl.BlockSpec((tm, SEG), row)
    tile = pl.BlockSpec((tm, LANES), row)
    return pl.pallas_call(
        _mix_out_kernel,
        grid=(m // tm,),
        in_specs=[seg, seg, tile, tile, tile, tile, tile, tile,
                  pl.BlockSpec((tm, D_MODEL), row),
                  pl.BlockSpec((3 * SEG, D_MODEL), lambda i: (0, 0))],
        out_specs=pl.BlockSpec((tm, D_MODEL), row),
        out_shape=jax.ShapeDtypeStruct((m, D_MODEL), F32),
        compiler_params=_cparams("parallel"),
    )(oh, om, os_[0], os_[1], os_[2], ls_[0], ls_[1], ls_[2], x, w)


def _ffn_kernel(x_ref, g_ref, wg_ref, wu_ref, wd_ref, fg_ref, o_ref, h_ref, acc_ref, *, final):
    j = pl.program_id(1)

    @pl.when(j == 0)
    def _():
        x = x_ref[...]
        h_ref[...] = _rmsnorm(x, g_ref[...]).astype(BF16)
        acc_ref[...] = x

    h = h_ref[...]
    a = _dot(h, wg_ref[...])
    u = _dot(h, wu_ref[...])
    acc_ref[...] += _dot((a * _sigmoid(a) * u).astype(BF16), wd_ref[...])

    @pl.when(j == pl.num_programs(1) - 1)
    def _():
        y = acc_ref[...]
        o_ref[...] = _rmsnorm(y, fg_ref[...]) if final else y


def ffn_dense(x, g, wg, wu, wd, fg, final, tm, tf):
    m = x.shape[0]
    ff = wg.shape[1]
    return pl.pallas_call(
        functools.partial(_ffn_kernel, final=final),
        grid=(m // tm, ff // tf),
        in_specs=[pl.BlockSpec((tm, D_MODEL), lambda i, j: (i, 0)),
                  pl.BlockSpec((1, D_MODEL), lambda i, j: (0, 0)),
                  pl.BlockSpec((D_MODEL, tf), lambda i, j: (0, j)),
                  pl.BlockSpec((D_MODEL, tf), lambda i, j: (0, j)),
                  pl.BlockSpec((tf, D_MODEL), lambda i, j: (j, 0)),
                  pl.BlockSpec((1, D_MODEL), lambda i, j: (0, 0))],
        out_specs=pl.BlockSpec((tm, D_MODEL), lambda i, j: (i, 0)),
        out_shape=jax.ShapeDtypeStruct((m, D_MODEL), F32),
        scratch_shapes=[pltpu.VMEM((tm, D_MODEL), BF16), pltpu.VMEM((tm, D_MODEL), F32)],
        compiler_params=_cparams("parallel", "arbitrary"),
    )(x, g, wg, wu, wd, fg)


def _moe_kernel(x_ref, g_ref, rw_hi_ref, rw_lo_ref, rb_ref, wg_ref, wu_ref, wd_ref, fg_ref, o_ref,
                h_ref, acc_ref, gate_ref, *, final):
    e = pl.program_id(1)
    tm = x_ref.shape[0]
    lane = _iota((tm, LANES), 1)

    @pl.when(e == 0)
    def _():
        x = x_ref[...]
        hf = _rmsnorm(x, g_ref[...])
        h_ref[...] = hf.astype(BF16)
        acc_ref[...] = x
        h_hi, h_lo = _split(hf, 2)
        logits = (_dot(h_hi, rw_hi_ref[...]) + _dot(h_lo, rw_hi_ref[...]) + _dot(h_hi, rw_lo_ref[...])
                  + rb_ref[...])
        lanef = lane.astype(F32)
        lg = jnp.where(lane < N_EXPERTS, logits, -jnp.inf)
        m1 = jnp.max(lg, axis=-1, keepdims=True)
        i1 = jnp.min(jnp.where(lg == m1, lanef, float(LANES)), axis=-1, keepdims=True)
        lg2 = jnp.where(lanef == i1, -jnp.inf, lg)
        m2 = jnp.max(lg2, axis=-1, keepdims=True)
        i2 = jnp.min(jnp.where(lg2 == m2, lanef, float(LANES)), axis=-1, keepdims=True)
        e2 = jnp.exp(m2 - m1)
        w1 = 1.0 / (1.0 + e2)
        gate_ref[...] = jnp.where(lanef == i1, w1, 0.0) + jnp.where(lanef == i2, e2 * w1, 0.0)

    ge = jnp.sum(jnp.where(lane == e, gate_ref[...], 0.0), axis=-1, keepdims=True)
    h = h_ref[...]
    a = _dot(h, wg_ref[...])
    u = _dot(h, wu_ref[...])
    acc_ref[...] += ge * _dot((a * _sigmoid(a) * u).astype(BF16), wd_ref[...])

    @pl.when(e == pl.num_programs(1) - 1)
    def _():
        y = acc_ref[...]
        o_ref[...] = _rmsnorm(y, fg_ref[...]) if final else y


def ffn_moe(x, g, rw_hi, rw_lo, rb, wg, wu, wd, fg, final, tm):
    m = x.shape[0]
    ff = wg.shape[2]
    c2 = lambda i, e: (0, 0)
    return pl.pallas_call(
        functools.partial(_moe_kernel, final=final),
        grid=(m // tm, N_EXPERTS),
        in_specs=[pl.BlockSpec((tm, D_MODEL), lambda i, e: (i, 0)),
                  pl.BlockSpec((1, D_MODEL), c2),
                  pl.BlockSpec((D_MODEL, LANES), c2),
                  pl.BlockSpec((D_MODEL, LANES), c2),
                  pl.BlockSpec((1, LANES), c2),
                  pl.BlockSpec((None, D_MODEL, ff), lambda i, e: (e, 0, 0)),
                  pl.BlockSpec((None, D_MODEL, ff), lambda i, e: (e, 0, 0)),
                  pl.BlockSpec((None, ff, D_MODEL), lambda i, e: (e, 0, 0)),
                  pl.BlockSpec((1, D_MODEL), c2)],
        out_specs=pl.BlockSpec((tm, D_MODEL), lambda i, e: (i, 0)),
        out_shape=jax.ShapeDtypeStruct((m, D_MODEL), F32),
        scratch_shapes=[pltpu.VMEM((tm, D_MODEL), BF16), pltpu.VMEM((tm, D_MODEL), F32),
                        pltpu.VMEM((tm, LANES), F32)],
        compiler_params=_cparams("parallel", "arbitrary"),
    )(x, g, rw_hi, rw_lo, rb, wg, wu, wd, fg)


def _flat_selectors():
    ek = np.zeros((2, LANES, FLAT), np.float32)
    tv = np.zeros((2, LANES, FLAT), np.float32)
    for j in range(2):
        for i in range(HEAD):
            ek[j, j * HEAD + i, i * HEAD:(i + 1) * HEAD] = 1.0
            tv[j, j * HEAD + i, i::HEAD] = 1.0
    rv = np.transpose(tv, (0, 2, 1)).copy()
    return jnp.asarray(ek, BF16), jnp.asarray(tv, BF16), jnp.asarray(rv, BF16)


def _hgrn_step_kernel(p_ref, lb_ref, g_ref, ek_ref, tv_ref, rv_ref, s_ref, o_ref, s_out_ref):
    hq = p_ref[:, OFF_HQ:OFF_HQ + SEG]
    z = p_ref[:, OFF_HF:OFF_HF + SEG]
    v = p_ref[:, OFF_HI:OFF_HI + SEG]
    hg = p_ref[:, OFF_HG:OFF_HG + SEG]
    logf, kk, q = _hgrn_gates(hq, z, lb_ref[...])
    dec = jnp.exp(logf)
    outs = []
    for p in range(N_PAIRS):
        sl = slice(p * LANES, (p + 1) * LANES)
        o = jnp.zeros((p_ref.shape[0], LANES), F32)
        for j in range(2):
            h = 2 * p + j
            if h >= N_HEADS:
                continue
            fl = slice(h * FLAT, (h + 1) * FLAT)
            ek, tv = ek_ref[j], tv_ref[j]
            s_new = _dot_sel(dec[:, sl], ek) * s_ref[:, fl] + _dot_sel(kk[:, sl], ek, 2) * _dot_sel(v[:, sl], tv, 2)
            s_out_ref[:, fl] = s_new
            o = o + _dot_sel(_dot_sel(q[:, sl], ek, 2) * s_new, rv_ref[j], 2)
        outs.append(o)
    o = jnp.concatenate(outs, axis=1)
    o_ref[...] = _head_rms(o, g_ref[...]) * (hg * _sigmoid(hg))


def hgrn_step(proj, lb, g, state, nb):
    n = proj.shape[0]
    ek, tv, rv = _flat_selectors()
    c2 = lambda i: (0, 0)
    c3 = lambda i: (0, 0, 0)
    return pl.pallas_call(
        _hgrn_step_kernel,
        grid=(n // nb,),
        in_specs=[pl.BlockSpec((nb, 4 * SEG), lambda i: (i, 0)),
                  pl.BlockSpec((1, SEG), c2), pl.BlockSpec((1, SEG), c2),
                  pl.BlockSpec((2, LANES, FLAT), c3), pl.BlockSpec((2, LANES, FLAT), c3),
                  pl.BlockSpec((2, FLAT, LANES), c3),
                  pl.BlockSpec((nb, N_HEADS * FLAT), lambda i: (i, 0))],
        out_specs=[pl.BlockSpec((nb, SEG), lambda i: (i, 0)),
                   pl.BlockSpec((nb, N_HEADS * FLAT), lambda i: (i, 0))],
        out_shape=[jax.ShapeDtypeStruct((n, SEG), F32), jax.ShapeDtypeStruct(state.shape, F32)],
        compiler_params=_cparams("parallel"),
    )(proj, lb, g, ek, tv, rv, state)


def _mlstm_step_kernel(mu_ref, mv_ref, mo_ref, gate_ref, conv_ref, cw_ref, cb_ref, wq_ref, wk_ref, gb_ref,
                       g_ref, xig_ref, xf_ref, ek_ref, tv_ref, rv_ref, c_ref, n_ref, m_ref,
                       o_ref, c_out_ref, n_out_ref, m_out_ref):
    nb = mu_ref.shape[0]
    y = cb_ref[...] + cw_ref[CONV_W - 1:CONV_W, :] * mu_ref[...]
    for j in range(CONV_W - 1):
        y = y + cw_ref[j:j + 1, :] * conv_ref[j]
    cact = (y * _sigmoid(y)).astype(BF16)
    q = jnp.concatenate([_dot(cact[:, p * LANES:(p + 1) * LANES], wq_ref[p]) for p in range(N_PAIRS)], axis=1)
    k = jnp.concatenate([_dot(cact[:, p * LANES:(p + 1) * LANES], wk_ref[p]) for p in range(N_PAIRS)], axis=1)
    v = mv_ref[...]

    gfull = gate_ref[...] + gb_ref[...]
    lane = _iota((nb, LANES), 1)
    gv = jnp.where(lane < N_HEADS, gfull, _log_sigmoid(gfull))
    igx = _dot_sel(gv, xig_ref[...])
    fx = _dot_sel(gv, xf_ref[...])
    mx = _dot_sel(m_ref[...], xig_ref[...])
    li = fx + mx
    mt = jnp.maximum(li, igx)
    dw = jnp.exp(igx - mt)
    ai = jnp.exp(li - mt)
    kw = k * dw
    n_new = ai * n_ref[...] + kw
    n_out_ref[...] = n_new
    m_out_ref[...] = mt
    nq = _pair_sums(q * n_new)

    nums = []
    for p in range(N_PAIRS):
        sl = slice(p * LANES, (p + 1) * LANES)
        num = jnp.zeros((nb, LANES), F32)
        for j in range(2):
            h = 2 * p + j
            if h >= N_HEADS:
                continue
            fl = slice(h * FLAT, (h + 1) * FLAT)
            ek, tv = ek_ref[j], tv_ref[j]
            c_new = _dot_sel(ai[:, sl], ek) * c_ref[:, fl] + _dot_sel(kw[:, sl], ek, 2) * _dot_sel(v[:, sl], tv, 2)
            c_out_ref[:, fl] = c_new
            num = num + _dot_sel(_dot_sel(q[:, sl], ek, 2) * c_new, rv_ref[j], 2)
        nums.append(num)
    num = jnp.concatenate(nums, axis=1)
    hh = num / jnp.maximum(jnp.abs(nq), jnp.exp(-mt))
    o_ref[...] = _head_rms(hh, g_ref[...]) * _sigmoid(mo_ref[...])


def mlstm_step(proj, conv, cw, cb, wq, wk, gb, g, c_state, n_state, m_state, nb):
    n = proj.shape[0]
    x_ig, x_f, _, _ = _gate_selectors()
    ek, tv, rv = _flat_selectors()
    c2 = lambda i: (0, 0)
    c3 = lambda i: (0, 0, 0)
    seg = pl.BlockSpec((nb, SEG), lambda i: (i, 0))
    flat = pl.BlockSpec((nb, N_HEADS * FLAT), lambda i: (i, 0))
    return pl.pallas_call(
        _mlstm_step_kernel,
        grid=(n // nb,),
        in_specs=[pl.BlockSpec((nb, SEG), lambda i: (i, OFF_MU // SEG)),
                  pl.BlockSpec((nb, SEG), lambda i: (i, OFF_MV // SEG)),
                  pl.BlockSpec((nb, SEG), lambda i: (i, OFF_MO // SEG)),
                  pl.BlockSpec((nb, LANES), lambda i: (i, OFF_GATE // LANES)),
                  pl.BlockSpec((CONV_W - 1, nb, SEG), lambda i: (0, i, 0)),
                  pl.BlockSpec((8, SEG), c2), pl.BlockSpec((1, SEG), c2),
                  pl.BlockSpec((N_PAIRS, LANES, LANES), c3), pl.BlockSpec((N_PAIRS, LANES, LANES), c3),
                  pl.BlockSpec((1, LANES), c2), pl.BlockSpec((1, SEG), c2),
                  pl.BlockSpec((LANES, SEG), c2), pl.BlockSpec((LANES, SEG), c2),
                  pl.BlockSpec((2, LANES, FLAT), c3), pl.BlockSpec((2, LANES, FLAT), c3),
                  pl.BlockSpec((2, FLAT, LANES), c3),
                  flat, seg, pl.BlockSpec((nb, LANES), lambda i: (i, 0))],
        out_specs=[seg, flat, seg, seg],
        out_shape=[jax.ShapeDtypeStruct((n, SEG), F32), jax.ShapeDtypeStruct(c_state.shape, F32),
                   jax.ShapeDtypeStruct((n, SEG), F32), jax.ShapeDtypeStruct((n, SEG), F32)],
        compiler_params=_cparams("parallel"),
    )(proj, proj, proj, proj, conv, cw, cb, wq, wk, gb, g, x_ig, x_f, ek, tv, rv, c_state, n_state, m_state)


def _swa_step_kernel(q_ref, k_ref, v_ref, c0_ref, c1_ref, c2_ref, o_ref):
    nb = q_ref.shape[0]
    outs, lses = [], []
    for g, c_ref in enumerate((c0_ref, c1_ref, c2_ref)):
        sl = slice(g * LANES, (g + 1) * LANES)
        q = q_ref[:, sl] * (HEAD ** -0.5)
        kn = k_ref[:, sl]
        vn = v_ref[:, sl]
        prods = [c_ref[i, :, 0:LANES] * q[i:i + 1, :] for i in range(nb)]
        ssum = _pair_sums(jnp.concatenate(prods, axis=0))
        snew = _pair_sums(q * kn)
        o_rows, l_rows = [], []
        for i in range(nb):
            s = ssum[i * SWA_J:(i + 1) * SWA_J]
            sn = snew[i:i + 1, :]
            mx = jnp.maximum(jnp.max(s, axis=0, keepdims=True), sn)
            p = jnp.exp(s - mx)
            pn = jnp.exp(sn - mx)
            den = jnp.sum(p, axis=0, keepdims=True) + pn
            num = jnp.sum(p * c_ref[i, :, LANES:2 * LANES], axis=0, keepdims=True) + pn * vn[i:i + 1, :]
            o_rows.append(num / den)
            l_rows.append(mx + jnp.log(den))
        outs.append(jnp.concatenate(o_rows, axis=0))
        lses.append(jnp.concatenate(l_rows, axis=0))
    mx = jnp.maximum(jnp.maximum(lses[0], lses[1]), lses[2])
    es = [jnp.exp(l - mx) for l in lses]
    inv = 1.0 / (es[0] + es[1] + es[2])
    o_ref[...] = jnp.concatenate([outs[g] * (es[g] * inv) for g in range(N_GROUPS)], axis=1)


def swa_step(proj, keys, nb):
    n = proj.shape[0]
    cache = pl.BlockSpec((nb, SWA_J, 2 * LANES), lambda i: (i, 0, 0))
    return pl.pallas_call(
        _swa_step_kernel,
        grid=(n // nb,),
        in_specs=[pl.BlockSpec((nb, SEG), lambda i: (i, OFF_SQ // SEG)),
                  pl.BlockSpec((nb, SEG), lambda i: (i, OFF_SK // SEG)),
                  pl.BlockSpec((nb, SEG), lambda i: (i, OFF_SV // SEG)),
                  cache, cache, cache],
        out_specs=pl.BlockSpec((nb, SEG), lambda i: (i, 0)),
        out_shape=jax.ShapeDtypeStruct((n, SEG), F32),
        compiler_params=_cparams("parallel"),
    )(proj, proj, proj, keys[0], keys[1], keys[2])


def _mix_out_step_kernel(oh_ref, om_ref, os_ref, x_ref, w_ref, y_ref):
    mix = jnp.concatenate([oh_ref[...], om_ref[...], os_ref[...]], axis=1)
    y_ref[...] = x_ref[...] + _dot(mix.astype(BF16), w_ref[...])


def mix_out_step(oh, om, os_, x, w):
    m = x.shape[0]
    c2 = lambda i: (0, 0)
    seg = pl.BlockSpec((m, SEG), c2)
    return pl.pallas_call(
        _mix_out_step_kernel,
        grid=(1,),
        in_specs=[seg, seg, seg, pl.BlockSpec((m, D_MODEL), c2), pl.BlockSpec((3 * SEG, D_MODEL), c2)],
        out_specs=pl.BlockSpec((m, D_MODEL), c2),
        out_shape=jax.ShapeDtypeStruct((m, D_MODEL), F32),
        compiler_params=_cparams("arbitrary"),
    )(oh, om, os_, x, w)


def _cache_shift_kernel(c0, c1, c2, r0, r1, r2, o0, o1, o2, sem):
    copies = []
    for g, (c, r, o) in enumerate(((c0, r0, o0), (c1, r1, o1), (c2, r2, o2))):
        w = c.shape[2]
        copies.append(pltpu.make_async_copy(c.at[:, :, pl.ds(1, w - 1)], o.at[:, :, pl.ds(0, w - 1)], sem.at[2 * g]))
        copies.append(pltpu.make_async_copy(r, o.at[:, :, pl.ds(w - 1, 1)], sem.at[2 * g + 1]))
    for cp in copies:
        cp.start()
    for cp in copies:
        cp.wait()


def cache_shift(caches, rows):
    any_spec = pl.BlockSpec(memory_space=pl.ANY)
    return pl.pallas_call(
        _cache_shift_kernel,
        in_specs=[any_spec] * 6,
        out_specs=[any_spec] * 3,
        out_shape=[jax.ShapeDtypeStruct(c.shape, c.dtype) for c in caches],
        scratch_shapes=[pltpu.SemaphoreType.DMA((6,))],
    )(*caches, *rows)


def _pad_cols(a, n):
    return jnp.pad(a, [(0, 0)] * (a.ndim - 1) + [(0, n - a.shape[-1])])


def _relayout_w_in(w):
    segs = jnp.split(w, np.cumsum(IN_SPLITS)[:-1].tolist(), axis=1)
    hq, hf, hi, hg, mu, mv, mo, mig, mfg, sq, sk, sv = segs
    cols = [_pad_cols(s, SEG) for s in (hq, hf, hi, hg, mu, mv, mo)] + [sq, sk, sv]
    cols.append(_pad_cols(jnp.concatenate([mig, mfg], axis=1), LANES))
    return jnp.concatenate(cols, axis=1).astype(BF16)


def _relayout_w_out(w):
    n5 = N_HEADS * HEAD
    rows = [w[:n5], jnp.zeros((SEG - n5, D_MODEL), w.dtype), w[n5:2 * n5], jnp.zeros((SEG - n5, D_MODEL), w.dtype),
            w[2 * n5:]]
    return jnp.concatenate(rows, axis=0).astype(BF16)


def _pair_block_diag(w):
    w6 = jnp.concatenate([w, jnp.zeros((1, HEAD, HEAD), w.dtype)], axis=0)
    z = jnp.zeros((HEAD, HEAD), w.dtype)
    return jnp.stack([jnp.block([[w6[2 * p], z], [z, w6[2 * p + 1]]]) for p in range(N_PAIRS)], axis=0)


def _row(a, n):
    return _pad_cols(a.reshape(1, -1).astype(F32), n)


def _unpair_state(st):
    heads = []
    for h in range(N_HEADS):
        p, j = divmod(h, 2)
        heads.append(st[:, p, j * HEAD:(j + 1) * HEAD, j * HEAD:(j + 1) * HEAD])
    return jnp.stack(heads, axis=1)


def kernel(x_prompt, x_sample, state_hgrn, state_mlstm_C, state_mlstm_n, state_mlstm_m, state_mlstm_conv,
           cache_swa_w128, cache_swa_w512, cache_swa_w2048, norm1_g, norm2_g, final_norm_g, w_in, w_out,
           hgrn_lb_param, hgrn_onorm_g, mlstm_conv_w, mlstm_conv_b, mlstm_wq, mlstm_wk, mlstm_ig_b,
           mlstm_fg_b, mlstm_onorm_g, ffn_w_gate, ffn_w_up, ffn_w_down, moe_router_w, moe_router_b,
           moe_w_gate, moe_w_up, moe_w_down):
    batch, seq, _ = x_prompt.shape
    n_dec = x_sample.shape[0]
    depth = w_in.shape[0]
    caches = (cache_swa_w128, cache_swa_w512, cache_swa_w2048)

    p_lb = jax.nn.softmax(hgrn_lb_param.astype(F32), axis=0)
    cs = jnp.cumsum(p_lb, axis=0)
    lb_all = cs - cs[0:1]

    xp = x_prompt.reshape(batch * seq, D_MODEL)
    xs = x_sample.reshape(n_dec, D_MODEL)
    out_p = {k: [] for k in ("hgrn", "C", "n", "m", "conv", "swa0", "swa1", "swa2")}
    out_s = {k: [] for k in ("hgrn", "C", "n", "m", "conv", "row0", "row1", "row2")}

    for li in range(depth):
        w_in_l = _relayout_w_in(w_in[li])
        w_out_l = _relayout_w_out(w_out[li])
        n1 = norm1_g[li].reshape(1, D_MODEL)
        n2 = norm2_g[li].reshape(1, D_MODEL)
        lb = jnp.pad(lb_all[li].reshape(1, -1), ((0, 0), (0, SEG - N_HEADS * HEAD)), constant_values=0.5)
        hg_g = _row(hgrn_onorm_g[li], SEG)
        ml_g = _row(mlstm_onorm_g[li], SEG)
        cw = jnp.pad(mlstm_conv_w[li].astype(F32), ((0, 8 - CONV_W), (0, SEG - N_HEADS * HEAD)))
        cb = _row(mlstm_conv_b[li], SEG)
        wq = _pair_block_diag(mlstm_wq[li]).astype(BF16)
        wk = (_pair_block_diag(mlstm_wk[li]) * (HEAD ** -0.5)).astype(BF16)
        gb = _row(jnp.concatenate([mlstm_ig_b[li], mlstm_fg_b[li]]), LANES)

        proj = norm_proj(xp, n1, w_in_l, 256)
        oh, s_h = hgrn_prompt(proj, lb, hg_g, batch, seq)
        om, s_c, s_n, s_m = mlstm_prompt(proj, cw, cb, wq, wk, gb, ml_g, batch, seq)
        p3 = proj.reshape(batch, seq, N_PROJ)
        os_, ls_ = [], []
        for g, (win, dil) in enumerate(SWA_PAIRS):
            def strided(off):
                a = p3[:, :, off + g * LANES:off + (g + 1) * LANES]
                return a.reshape(batch, seq // dil, dil, LANES).transpose(0, 2, 1, 3)
            o, lse = swa_prompt(strided(OFF_SQ), strided(OFF_SK), strided(OFF_SV))
            os_.append(o.transpose(0, 2, 1, 3).reshape(batch * seq, LANES))
            ls_.append(lse.transpose(0, 2, 1, 3).reshape(batch * seq, LANES))
            keep = min(win, seq)
            kv = jnp.stack([p3[:, seq - keep:, OFF_SK + g * LANES:OFF_SK + (g + 1) * LANES],
                            p3[:, seq - keep:, OFF_SV + g * LANES:OFF_SV + (g + 1) * LANES]], axis=2)
            out_p["swa%d" % g].append(kv.reshape(batch, keep, 2, 2, HEAD))
        xp = mix_out(oh, om, os_, ls_, xp, w_out_l, 512)
        out_p["hgrn"].append(_unpair_state(s_h))
        out_p["C"].append(_unpair_state(s_c))
        out_p["n"].append(jnp.stack([s_n[:, h // 2, (h % 2) * HEAD:(h % 2 + 1) * HEAD, (h % 2) * HEAD]
                                     for h in range(N_HEADS)], axis=1))
        out_p["m"].append(s_m[:, 0, 0:N_HEADS * HEAD:HEAD])
        out_p["conv"].append(p3[:, seq - (CONV_W - 1):, OFF_MU:OFF_MU + N_HEADS * HEAD])

        projs = norm_proj(xs, n1, w_in_l, n_dec)
        ohs, hs_new = hgrn_step(projs, lb, hg_g, state_hgrn[li].reshape(n_dec, N_HEADS * FLAT), 32)
        conv_in = _pad_cols(state_mlstm_conv[li].astype(F32), SEG).transpose(1, 0, 2)
        oms, c_new, n_new, m_new = mlstm_step(
            projs, conv_in, cw, cb, wq, wk, gb, ml_g,
            state_mlstm_C[li].reshape(n_dec, N_HEADS * FLAT),
            _pad_cols(state_mlstm_n[li].reshape(n_dec, N_HEADS * HEAD), SEG),
            _pad_cols(state_mlstm_m[li], LANES), 32)
        keys = [caches[g][li][:, ::dil].reshape(n_dec, SWA_J, 2 * LANES) for g, (_, dil) in enumerate(SWA_PAIRS)]
        oss = swa_step(projs, keys, 8)
        xs = mix_out_step(ohs, oms, oss, xs, w_out_l)
        out_s["hgrn"].append(hs_new.reshape(n_dec, N_HEADS, HEAD, HEAD))
        out_s["C"].append(c_new.reshape(n_dec, N_HEADS, HEAD, HEAD))
        out_s["n"].append(n_new[:, :N_HEADS * HEAD].reshape(n_dec, N_HEADS, HEAD))
        out_s["m"].append(m_new[:, 0:N_HEADS * HEAD:HEAD])
        out_s["conv"].append(jnp.concatenate(
            [state_mlstm_conv[li][:, 1:], projs[:, None, OFF_MU:OFF_MU + N_HEADS * HEAD]], axis=1))
        for g in range(N_GROUPS):
            row = jnp.stack([projs[:, OFF_SK + g * LANES:OFF_SK + (g + 1) * LANES],
                             projs[:, OFF_SV + g * LANES:OFF_SV + (g + 1) * LANES]], axis=1)
            out_s["row%d" % g].append(row.reshape(n_dec, 1, 2, 2, HEAD))

        final = li == depth - 1
        fg = final_norm_g.reshape(1, D_MODEL)
        j = li // 2
        if li % 2 == 0:
            wg, wu, wd = (ffn_w_gate[j].astype(BF16), ffn_w_up[j].astype(BF16), ffn_w_down[j].astype(BF16))
            xp = ffn_dense(xp, n2, wg, wu, wd, fg, final, 512, 1408)
            xs = ffn_dense(xs, n2, wg, wu, wd, fg, final, n_dec, 1408)
        else:
            rw = _pad_cols(moe_router_w[j].astype(F32), LANES)
            rw_hi = rw.astype(BF16)
            rw_lo = (rw - rw_hi.astype(F32)).astype(BF16)
            rb = _row(moe_router_b[j], LANES)
            wg, wu, wd = (moe_w_gate[j].astype(BF16), moe_w_up[j].astype(BF16), moe_w_down[j].astype(BF16))
            xp = ffn_moe(xp, n2, rw_hi, rw_lo, rb, wg, wu, wd, fg, final, 512)
            xs = ffn_moe(xs, n2, rw_hi, rw_lo, rb, wg, wu, wd, fg, final, n_dec)

    new_caches = cache_shift(caches, [jnp.stack(out_s["row%d" % g], axis=0) for g in range(N_GROUPS)])
    stk = lambda lst: jnp.stack(lst, axis=0)
    return (xp.reshape(batch, seq, D_MODEL), xs.reshape(n_dec, 1, D_MODEL),
            stk(out_p["hgrn"]), stk(out_s["hgrn"]),
            stk(out_p["C"]), stk(out_s["C"]),
            stk(out_p["n"]), stk(out_s["n"]),
            stk(out_p["m"]), stk(out_s["m"]),
            stk(out_p["conv"]), stk(out_s["conv"]),
            stk(out_p["swa0"]), new_caches[0],
            stk(out_p["swa1"]), new_caches[1],
            stk(out_p["swa2"]), new_caches[2])
```

```python
import functools

import numpy as np
import jax
import jax.numpy as jnp
from jax import lax
from jax.experimental import pallas as pl
from jax.experimental.pallas import tpu as pltpu

F32 = jnp.float32
BF16 = jnp.bfloat16

D_MODEL = 1024
HEAD = 64
N_HEADS = 5
N_PAIRS = 3
SEG = 384
LANES = 128
FLAT = HEAD * HEAD
CONV_W = 4
SWA_PAIRS = ((128, 1), (512, 4), (2048, 16))
SWA_J = 128
N_GROUPS = 3
EPS = 1e-6
N_EXPERTS = 8
IN_SPLITS = (320, 320, 320, 320, 320, 320, 320, 5, 5, 384, 384, 384)

OFF_HQ, OFF_HF, OFF_HI, OFF_HG = 0, 384, 768, 1152
OFF_MU, OFF_MV, OFF_MO = 1536, 1920, 2304
OFF_SQ, OFF_SK, OFF_SV = 2688, 3072, 3456
OFF_GATE = 3840
N_PROJ = 3968

CHUNK = 128
VMEM_LIMIT = 56 * 1024 * 1024


def _cparams(*sem):
    return pltpu.CompilerParams(dimension_semantics=sem, vmem_limit_bytes=VMEM_LIMIT)


def _dot(a, b):
    return jnp.dot(a, b, preferred_element_type=F32)


def _dot_nt(a, b):
    return lax.dot_general(a, b, (((1,), (1,)), ((), ())), preferred_element_type=F32)


def _split(x, n):
    parts = []
    r = x
    for i in range(n):
        p = r.astype(BF16)
        parts.append(p)
        if i + 1 < n:
            r = r - p.astype(F32)
    return parts


def _dot_sel(x, sel, n=3):
    acc = None
    for p in _split(x, n):
        t = _dot(p, sel)
        acc = t if acc is None else acc + t
    return acc


def _sel_dot(sel, x, n=3):
    acc = None
    for p in _split(x, n):
        t = _dot(sel, p)
        acc = t if acc is None else acc + t
    return acc


def _sel_dot_nt(sel, x, n=3):
    acc = None
    for p in _split(x, n):
        t = _dot_nt(sel, p)
        acc = t if acc is None else acc + t
    return acc


def _sigmoid(x):
    return 1.0 / (1.0 + jnp.exp(-x))


def _log_sigmoid(x):
    return jnp.minimum(x, 0.0) - jnp.log1p(jnp.exp(-jnp.abs(x)))


def _iota(shape, axis):
    return lax.broadcasted_iota(jnp.int32, shape, axis)


def _tril_bf16(n):
    return jnp.where(_iota((n, n), 0) >= _iota((n, n), 1), 1.0, 0.0).astype(BF16)


def _same_head(n_rows, n_cols):
    return (_iota((n_rows, n_cols), 0) // HEAD) == (_iota((n_rows, n_cols), 1) // HEAD)


def _head_ones():
    return jnp.where(_same_head(LANES, LANES), 1.0, 0.0).astype(BF16)


def _stack_heads(x):
    lane = _iota(x.shape, 1)
    zero = jnp.zeros_like(x)
    return jnp.concatenate([jnp.where(lane < HEAD, x, zero).astype(BF16),
                            jnp.where(lane >= HEAD, x, zero).astype(BF16)], axis=0)


def _stack_heads_ones(x):
    lane = _iota(x.shape, 1)
    zero = jnp.zeros_like(x)
    one = jnp.ones_like(x)
    top = jnp.concatenate([jnp.where(lane < HEAD, x, zero), jnp.where(lane < HEAD, one, zero)], axis=1)
    bot = jnp.concatenate([jnp.where(lane >= HEAD, x, zero), jnp.where(lane >= HEAD, one, zero)], axis=1)
    return jnp.concatenate([top.astype(BF16), bot.astype(BF16)], axis=0)


def _pair_sums(x, n=2):
    ones = _head_ones()
    cols = [_dot_sel(x[:, p * LANES:(p + 1) * LANES], ones, n) for p in range(x.shape[1] // LANES)]
    return cols[0] if len(cols) == 1 else jnp.concatenate(cols, axis=1)


def _head_rms(o, g):
    msq = _pair_sums(o * o) * (1.0 / HEAD)
    return o * lax.rsqrt(msq + EPS) * g


def _rmsnorm(x, g):
    return x * lax.rsqrt(jnp.mean(x * x, axis=-1, keepdims=True) + EPS) * g


def _norm_proj_kernel(x_ref, g_ref, w_ref, o_ref):
    o_ref[...] = _dot(_rmsnorm(x_ref[...], g_ref[...]).astype(BF16), w_ref[...])


def norm_proj(x, g, w, tm):
    m = x.shape[0]
    n = w.shape[1]
    return pl.pallas_call(
        _norm_proj_kernel,
        grid=(m // tm,),
        in_specs=[pl.BlockSpec((tm, D_MODEL), lambda i: (i, 0)),
                  pl.BlockSpec((1, D_MODEL), lambda i: (0, 0)),
                  pl.BlockSpec((D_MODEL, n), lambda i: (0, 0))],
        out_specs=pl.BlockSpec((tm, n), lambda i: (i, 0)),
        out_shape=jax.ShapeDtypeStruct((m, n), F32),
        compiler_params=_cparams("parallel"),
    )(x, g, w)


def _hgrn_level_map(L):
    t = np.arange(L)[:, None]
    s = np.arange(L)[None, :]
    lev = np.full((L, L), -1, np.int32)
    n_lev = int(np.log2(L))
    for k in range(n_lev):
        h = L >> (k + 1)
        ok = (t // (2 * h) == s // (2 * h)) & (t % (2 * h) >= h) & (s % (2 * h) < h)
        lev[ok] = k
    lev[t == s] = n_lev
    return np.concatenate([lev, lev], axis=0)


def _block_ref(b, h):
    L = b.shape[0]
    if 2 * h >= 8:
        pieces = []
        for j in range(L // (2 * h)):
            r0 = j * 2 * h + h - 1
            pieces.append(jnp.broadcast_to(b[r0:r0 + 1, :], (2 * h, b.shape[1])))
        return jnp.concatenate(pieces, axis=0)
    pos = _iota(b.shape, 0) % (2 * h)
    out = b
    for d in range(-(h - 1), h + 1):
        if d != 0:
            out = jnp.where(pos == h - 1 + d, pltpu.roll(b, d % L, axis=0), out)
    return out


def _hgrn_gates(hq, z, lb):
    la = jnp.log(lb)
    lc = jnp.log1p(-lb) + _log_sigmoid(z)
    logf = jnp.maximum(la, lc) + jnp.log1p(jnp.exp(-jnp.abs(la - lc)))
    kk = (1.0 - lb) * _sigmoid(-z)
    q = hq * _sigmoid(hq)
    return logf, kk, q


def _hgrn_kernel(p_ref, lb_ref, g_ref, lev_ref, o_ref, s_out_ref, s_ref):
    c = pl.program_id(1)
    L = p_ref.shape[0]

    @pl.when(c == 0)
    def _():
        s_ref[...] = jnp.zeros_like(s_ref)

    hq = p_ref[:, OFF_HQ:OFF_HQ + SEG]
    z = p_ref[:, OFF_HF:OFF_HF + SEG]
    v = p_ref[:, OFF_HI:OFF_HI + SEG]
    hg = p_ref[:, OFF_HG:OFF_HG + SEG]
    logf, kk, q = _hgrn_gates(hq, z, lb_ref[...])

    b = _sel_dot(_tril_bf16(L), logf)
    b_last = b[L - 1:L, :]

    lev = lev_ref[...]
    n_lev = int(np.log2(L))
    row = _iota((L, SEG), 0)
    acc = [jnp.zeros((2 * L, L), F32) for _ in range(N_PAIRS)]
    for k in range(n_lev + 1):
        if k < n_lev:
            h = L >> (k + 1)
            ex = jnp.exp(-jnp.abs(b - _block_ref(b, h)))
            xl = jnp.where(row % (2 * h) >= h, q, kk) * ex
            xr = xl
        else:
            xl = q
            xr = kk
        for p in range(N_PAIRS):
            sl = slice(p * LANES, (p + 1) * LANES)
            sc = _dot_nt(_stack_heads(xl[:, sl]), xr[:, sl].astype(BF16))
            acc[p] = jnp.where(lev == k, sc, acc[p])

    qd = (q * jnp.exp(b)).astype(BF16)
    kd = kk * jnp.exp(b_last - b)
    vb = v.astype(BF16)
    same = _same_head(LANES, LANES)
    outs = []
    for p in range(N_PAIRS):
        sl = slice(p * LANES, (p + 1) * LANES)
        a2 = jnp.concatenate([acc[p][:L], acc[p][L:]], axis=1).astype(BF16)
        s_old = s_ref[p]
        outs.append(_dot(a2, _stack_heads(v[:, sl])) + _dot(qd[:, sl], s_old.astype(BF16)))
        dec = jnp.broadcast_to(jnp.exp(b_last[:, sl]), (LANES, LANES)).T
        kv = _dot(kd[:, sl].T.astype(BF16), vb[:, sl])
        s_ref[p] = dec * s_old + jnp.where(same, kv, 0.0)
    o = jnp.concatenate(outs, axis=1)
    o_ref[...] = _head_rms(o, g_ref[...]) * (hg * _sigmoid(hg))

    @pl.when(c == pl.num_programs(1) - 1)
    def _():
        s_out_ref[...] = s_ref[...]


def hgrn_prompt(proj, lb, g, batch, seq):
    L = CHUNK
    nc = seq // L
    lev = jnp.asarray(_hgrn_level_map(L))
    return pl.pallas_call(
        _hgrn_kernel,
        grid=(batch, nc),
        in_specs=[pl.BlockSpec((L, 4 * SEG), lambda b, c: (b * nc + c, 0)),
                  pl.BlockSpec((1, SEG), lambda b, c: (0, 0)),
                  pl.BlockSpec((1, SEG), lambda b, c: (0, 0)),
                  pl.BlockSpec((2 * L, L), lambda b, c: (0, 0))],
        out_specs=[pl.BlockSpec((L, SEG), lambda b, c: (b * nc + c, 0)),
                   pl.BlockSpec((None, N_PAIRS, LANES, LANES), lambda b, c: (b, 0, 0, 0))],
        out_shape=[jax.ShapeDtypeStruct((batch * seq, SEG), F32),
                   jax.ShapeDtypeStruct((batch, N_PAIRS, LANES, LANES), F32)],
        scratch_shapes=[pltpu.VMEM((N_PAIRS, LANES, LANES), F32)],
        compiler_params=_cparams("parallel", "arbitrary"),
    )(proj, lb, g, lev)


def _gate_selectors():
    x_ig = np.zeros((LANES, SEG), np.float32)
    x_f = np.zeros((LANES, SEG), np.float32)
    x_ft = np.zeros((LANES, N_HEADS * LANES), np.float32)
    sel_d = np.zeros((8, LANES), np.float32)
    for h in range(N_HEADS):
        x_ig[h, h * HEAD:(h + 1) * HEAD] = 1.0
        x_f[N_HEADS + h, h * HEAD:(h + 1) * HEAD] = 1.0
        x_ft[N_HEADS + h, h * LANES:(h + 1) * LANES] = 1.0
        sel_d[h, h] = 1.0
        sel_d[h, N_HEADS + h] = -1.0
    return (jnp.asarray(x_ig, BF16), jnp.asarray(x_f, BF16), jnp.asarray(x_ft, BF16),
            jnp.asarray(sel_d, BF16))


def _mlstm_kernel(mu_ref, mv_ref, mo_ref, gate_ref, cw_ref, cb_ref, wq_ref, wk_ref, gb_ref, g_ref,
                  xig_ref, xf_ref, xft_ref, seld_ref,
                  o_ref, c_out_ref, n_out_ref, m_out_ref,
                  u_ref, c_ref, n_ref, m1_ref, m2_ref):
    ci = pl.program_id(1)
    L = mu_ref.shape[0]

    @pl.when(ci == 0)
    def _():
        u_ref[0:8, :] = jnp.zeros((8, SEG), F32)
        c_ref[...] = jnp.zeros_like(c_ref)
        n_ref[...] = jnp.zeros_like(n_ref)
        m1_ref[...] = jnp.zeros_like(m1_ref)
        m2_ref[...] = jnp.zeros_like(m2_ref)

    @pl.when(ci > 0)
    def _():
        u_ref[0:8, :] = u_ref[L:L + 8, :]

    u_ref[8:L + 8, :] = mu_ref[...]

    y = cb_ref[...] + cw_ref[CONV_W - 1:CONV_W, :] * u_ref[8:L + 8, :]
    for j in range(1, CONV_W):
        y = y + cw_ref[CONV_W - 1 - j:CONV_W - j, :] * u_ref[8 - j:8 - j + L, :]
    cact = (y * _sigmoid(y)).astype(BF16)
    v = mv_ref[...]
    vb = v.astype(BF16)

    gfull = gate_ref[...] + gb_ref[...]
    lane = _iota((L, LANES), 1)
    fcum = _sel_dot(_tril_bf16(L), _log_sigmoid(gfull))
    gv = jnp.where(lane < N_HEADS, gfull, fcum)
    igx = _dot_sel(gv, xig_ref[...])
    fx = _dot_sel(gv, xf_ref[...])
    ftile = _dot_sel(gv, xft_ref[...])
    rowv = _sel_dot_nt(seld_ref[...], gv)
    m1 = m1_ref[...]
    m2 = m2_ref[...]

    causal = _iota((L, L), 0) >= _iota((L, L), 1)
    lane_p = _iota((L, LANES), 1)
    same = _same_head(LANES, LANES)
    ones_l = jnp.ones((L, LANES), BF16)

    dws, mtt = [], []
    for h in range(N_HEADS):
        ft = ftile[:, h * LANES:(h + 1) * LANES]
        logd = jnp.where(causal, ft + rowv[h:h + 1, :], -jnp.inf)
        mt = jnp.maximum(ft + m2[:, h * LANES:(h + 1) * LANES], jnp.max(logd, axis=-1, keepdims=True))
        dws.append(jnp.exp(logd - mt))
        mtt.append(mt)
    m2_ref[...] = jnp.concatenate([mt[L - 1:L, :] for mt in mtt], axis=1)
    zeros_t = jnp.zeros((L, LANES), F32)
    mtx = jnp.concatenate([jnp.where(lane_p < HEAD, mtt[0], mtt[1]),
                           jnp.where(lane_p < HEAD, mtt[2], mtt[3]),
                           jnp.where(lane_p < HEAD, mtt[4], zeros_t)], axis=1)
    a_int = jnp.exp(fx + m1 - mtx)
    m1_new = mtx[L - 1:L, :]
    w = jnp.exp(fx[L - 1:L, :] - fx + igx - m1_new)
    a_c = jnp.exp(fx[L - 1:L, :] + m1 - m1_new)
    m1_ref[...] = m1_new

    nums, nqs = [], []
    for p in range(N_PAIRS):
        sl = slice(p * LANES, (p + 1) * LANES)
        cp = cact[:, sl]
        q = _dot(cp, wq_ref[p])
        k = _dot(cp, wk_ref[p])
        sc = _dot_nt(_stack_heads(q), k.astype(BF16))
        p0 = sc[:L] * dws[2 * p]
        p1 = sc[L:] * dws[2 * p + 1] if 2 * p + 1 < N_HEADS else jnp.zeros((L, L), F32)
        a2 = jnp.concatenate([p0, p1], axis=1).astype(BF16)
        intra = _dot(a2, _stack_heads_ones(v[:, sl]))
        c_old = c_ref[p]
        n_old = n_ref[p]
        inter = _dot(q.astype(BF16), jnp.concatenate([c_old, n_old], axis=1).astype(BF16))
        ai = a_int[:, sl]
        nums.append(intra[:, :LANES] + ai * inter[:, :LANES])
        nqs.append(intra[:, LANES:] + ai * inter[:, LANES:])
        kw = k * w[:, sl]
        kw_hi, kw_lo = _split(kw.T, 2)
        upd = _dot(kw_hi, jnp.concatenate([vb[:, sl], ones_l], axis=1))
        nsum = upd[:, LANES:] + _dot(kw_lo, ones_l)
        acp = a_c[:, sl]
        c_ref[p] = acp * c_old + jnp.where(same, upd[:, :LANES], 0.0)
        n_ref[p] = acp * n_old + jnp.where(same, nsum, 0.0)
    num = jnp.concatenate(nums, axis=1)
    nq = jnp.concatenate(nqs, axis=1)
    hh = num / jnp.maximum(jnp.abs(nq), jnp.exp(-mtx))
    o_ref[...] = _head_rms(hh, g_ref[...]) * _sigmoid(mo_ref[...])

    @pl.when(ci == pl.num_programs(1) - 1)
    def _():
        c_out_ref[...] = c_ref[...]
        n_out_ref[...] = n_ref[...]
        m_out_ref[...] = jnp.broadcast_to(m1_ref[...], (8, SEG))


def mlstm_prompt(proj, cw, cb, wq, wk, gb, g, batch, seq):
    L = CHUNK
    nc = seq // L
    x_ig, x_f, x_ft, sel_d = _gate_selectors()
    row = lambda b, c: (b * nc + c, 0)
    const2 = lambda b, c: (0, 0)
    const3 = lambda b, c: (0, 0, 0)
    st_spec = pl.BlockSpec((None, N_PAIRS, LANES, LANES), lambda b, c: (b, 0, 0, 0))
    st_shape = jax.ShapeDtypeStruct((batch, N_PAIRS, LANES, LANES), F32)
    return pl.pallas_call(
        _mlstm_kernel,
        grid=(batch, nc),
        in_specs=[pl.BlockSpec((L, SEG), lambda b, c: (b * nc + c, OFF_MU // SEG)),
                  pl.BlockSpec((L, SEG), lambda b, c: (b * nc + c, OFF_MV // SEG)),
                  pl.BlockSpec((L, SEG), lambda b, c: (b * nc + c, OFF_MO // SEG)),
                  pl.BlockSpec((L, LANES), lambda b, c: (b * nc + c, OFF_GATE // LANES)),
                  pl.BlockSpec((8, SEG), const2),
                  pl.BlockSpec((1, SEG), const2),
                  pl.BlockSpec((N_PAIRS, LANES, LANES), const3),
                  pl.BlockSpec((N_PAIRS, LANES, LANES), const3),
                  pl.BlockSpec((1, LANES), const2),
                  pl.BlockSpec((1, SEG), const2),
                  pl.BlockSpec((LANES, SEG), const2),
                  pl.BlockSpec((LANES, SEG), const2),
                  pl.BlockSpec((LANES, N_HEADS * LANES), const2),
                  pl.BlockSpec((8, LANES), const2)],
        out_specs=[pl.BlockSpec((L, SEG), row), st_spec, st_spec,
                   pl.BlockSpec((None, 8, SEG), lambda b, c: (b, 0, 0))],
        out_shape=[jax.ShapeDtypeStruct((batch * seq, SEG), F32), st_shape, st_shape,
                   jax.ShapeDtypeStruct((batch, 8, SEG), F32)],
        scratch_shapes=[pltpu.VMEM((L + 8, SEG), F32),
                        pltpu.VMEM((N_PAIRS, LANES, LANES), F32),
                        pltpu.VMEM((N_PAIRS, LANES, LANES), F32),
                        pltpu.VMEM((1, SEG), F32),
                        pltpu.VMEM((1, N_HEADS * LANES), F32)],
        compiler_params=_cparams("parallel", "arbitrary"),
    )(proj, proj, proj, proj, cw, cb, wq, wk, gb, g, x_ig, x_f, x_ft, sel_d)


def _swa_kernel(q_ref, kp_ref, kc_ref, vp_ref, vc_ref, o_ref, lse_ref):
    blk = pl.program_id(2)
    L = q_ref.shape[0]
    q = q_ref[...] * (HEAD ** -0.5)
    kcat = jnp.concatenate([kp_ref[...], kc_ref[...]], axis=0).astype(BF16)
    vcat = jnp.concatenate([vp_ref[...], vc_ref[...]], axis=0)
    s = _dot_nt(_stack_heads(q), kcat)
    qi = _iota((2 * L, 2 * L), 0) % L
    kj = _iota((2 * L, 2 * L), 1)
    dist = qi + L - kj
    valid = (dist >= 0) & (dist <= SWA_J) & (blk * L + kj - L >= 0)
    s = jnp.where(valid, s, -jnp.inf)
    mx = jnp.max(s, axis=-1, keepdims=True)
    p = jnp.exp(s - mx)
    p2 = jnp.concatenate([p[:L], p[L:]], axis=1).astype(BF16)
    r = _dot(p2, _stack_heads_ones(vcat))
    den = r[:, LANES:]
    lane = _iota((L, LANES), 1)
    mxt = jnp.where(lane < HEAD, mx[:L], mx[L:])
    o_ref[...] = r[:, :LANES] / den
    lse_ref[...] = mxt + jnp.log(den)


def swa_prompt(q, k, v):
    batch, dil, ls, _ = q.shape
    L = SWA_J
    nb = ls // L
    cur = pl.BlockSpec((None, None, L, LANES), lambda b, r, i: (b, r, i, 0))
    prev = pl.BlockSpec((None, None, L, LANES), lambda b, r, i: (b, r, jnp.maximum(i - 1, 0), 0))
    shape = jax.ShapeDtypeStruct(q.shape, F32)
    return pl.pallas_call(
        _swa_kernel,
        grid=(batch, dil, nb),
        in_specs=[cur, prev, cur, prev, cur],
        out_specs=[cur, cur],
        out_shape=[shape, shape],
        compiler_params=_cparams("parallel", "parallel", "arbitrary"),
    )(q, k, k, v, v)


def _mix_out_kernel(oh_ref, om_ref, o0_ref, o1_ref, o2_ref, l0_ref, l1_ref, l2_ref, x_ref, w_ref, y_ref):
    l0, l1, l2 = l0_ref[...], l1_ref[...], l2_ref[...]
    mx = jnp.maximum(jnp.maximum(l0, l1), l2)
    e0, e1, e2 = jnp.exp(l0 - mx), jnp.exp(l1 - mx), jnp.exp(l2 - mx)
    inv = 1.0 / (e0 + e1 + e2)
    mix = jnp.concatenate([oh_ref[...], om_ref[...], o0_ref[...] * (e0 * inv), o1_ref[...] * (e1 * inv),
                           o2_ref[...] * (e2 * inv)], axis=1)
    y_ref[...] = x_ref[...] + _dot(mix.astype(BF16), w_ref[...])


def mix_out(oh, om, os_, ls_, x, w, tm):
    m = x.shape[0]
    row = lambda i: (i, 0)
    seg = pl.BlockSpec((tm, SEG), row)
    tile = pl.BlockSpec((tm, LANES), row)
    return pl.pallas_call(
        _mix_out_kernel,
        grid=(m // tm,),
        in_specs=[seg, seg, tile, tile, tile, tile, tile, tile,
                  pl.BlockSpec((tm, D_MODEL), row),
                  pl.BlockSpec((3 * SEG, D_MODEL), lambda i: (0, 0))],
        out_specs=pl.BlockSpec((tm, D_MODEL), row),
        out_shape=jax.ShapeDtypeStruct((m, D_MODEL), F32),
        compiler_params=_cparams("parallel"),
    )(oh, om, os_[0], os_[1], os_[2], ls_[0], ls_[1], ls_[2], x, w)


def _ffn_kernel(x_ref, g_ref, wg_ref, wu_ref, wd_ref, fg_ref, o_ref, h_ref, acc_ref, *, final):
    j = pl.program_id(1)

    @pl.when(j == 0)
    def _():
        x = x_ref[...]
        h_ref[...] = _rmsnorm(x, g_ref[...]).astype(BF16)
        acc_ref[...] = x

    h = h_ref[...]
    a = _dot(h, wg_ref[...])
    u = _dot(h, wu_ref[...])
    acc_ref[...] += _dot((a * _sigmoid(a) * u).astype(BF16), wd_ref[...])

    @pl.when(j == pl.num_programs(1) - 1)
    def _():
        y = acc_ref[...]
        o_ref[...] = _rmsnorm(y, fg_ref[...]) if final else y


def ffn_dense(x, g, wg, wu, wd, fg, final, tm, tf):
    m = x.shape[0]
    ff = wg.shape[1]
    return pl.pallas_call(
        functools.partial(_ffn_kernel, final=final),
        grid=(m // tm, ff // tf),
        in_specs=[pl.BlockSpec((tm, D_MODEL), lambda i, j: (i, 0)),
                  pl.BlockSpec((1, D_MODEL), lambda i, j: (0, 0)),
                  pl.BlockSpec((D_MODEL, tf), lambda i, j: (0, j)),
                  pl.BlockSpec((D_MODEL, tf), lambda i, j: (0, j)),
                  pl.BlockSpec((tf, D_MODEL), lambda i, j: (j, 0)),
                  pl.BlockSpec((1, D_MODEL), lambda i, j: (0, 0))],
        out_specs=pl.BlockSpec((tm, D_MODEL), lambda i, j: (i, 0)),
        out_shape=jax.ShapeDtypeStruct((m, D_MODEL), F32),
        scratch_shapes=[pltpu.VMEM((tm, D_MODEL), BF16), pltpu.VMEM((tm, D_MODEL), F32)],
        compiler_params=_cparams("parallel", "arbitrary"),
    )(x, g, wg, wu, wd, fg)


def _moe_kernel(x_ref, g_ref, rw_hi_ref, rw_lo_ref, rb_ref, wg_ref, wu_ref, wd_ref, fg_ref, o_ref,
                h_ref, acc_ref, gate_ref, *, final):
    e = pl.program_id(1)
    tm = x_ref.shape[0]
    lane = _iota((tm, LANES), 1)

    @pl.when(e == 0)
    def _():
        x = x_ref[...]
        hf = _rmsnorm(x, g_ref[...])
        h_ref[...] = hf.astype(BF16)
        acc_ref[...] = x
        h_hi, h_lo = _split(hf, 2)
        logits = (_dot(h_hi, rw_hi_ref[...]) + _dot(h_lo, rw_hi_ref[...]) + _dot(h_hi, rw_lo_ref[...])
                  + rb_ref[...])
        lanef = lane.astype(F32)
        lg = jnp.where(lane < N_EXPERTS, logits, -jnp.inf)
        m1 = jnp.max(lg, axis=-1, keepdims=True)
        i1 = jnp.min(jnp.where(lg == m1, lanef, float(LANES)), axis=-1, keepdims=True)
        lg2 = jnp.where(lanef == i1, -jnp.inf, lg)
        m2 = jnp.max(lg2, axis=-1, keepdims=True)
        i2 = jnp.min(jnp.where(lg2 == m2, lanef, float(LANES)), axis=-1, keepdims=True)
        e2 = jnp.exp(m2 - m1)
        w1 = 1.0 / (1.0 + e2)
        gate_ref[...] = jnp.where(lanef == i1, w1, 0.0) + jnp.where(lanef == i2, e2 * w1, 0.0)

    ge = jnp.sum(jnp.where(lane == e, gate_ref[...], 0.0), axis=-1, keepdims=True)
    h = h_ref[...]
    a = _dot(h, wg_ref[...])
    u = _dot(h, wu_ref[...])
    acc_ref[...] += ge * _dot((a * _sigmoid(a) * u).astype(BF16), wd_ref[...])

    @pl.when(e == pl.num_programs(1) - 1)
    def _():
        y = acc_ref[...]
        o_ref[...] = _rmsnorm(y, fg_ref[...]) if final else y


def ffn_moe(x, g, rw_hi, rw_lo, rb, wg, wu, wd, fg, final, tm):
    m = x.shape[0]
    ff = wg.shape[2]
    c2 = lambda i, e: (0, 0)
    return pl.pallas_call(
        functools.partial(_moe_kernel, final=final),
        grid=(m // tm, N_EXPERTS),
        in_specs=[pl.BlockSpec((tm, D_MODEL), lambda i, e: (i, 0)),
                  pl.BlockSpec((1, D_MODEL), c2),
                  pl.BlockSpec((D_MODEL, LANES), c2),
                  pl.BlockSpec((D_MODEL, LANES), c2),
                  pl.BlockSpec((1, LANES), c2),
                  pl.BlockSpec((None, D_MODEL, ff), lambda i, e: (e, 0, 0)),
                  pl.BlockSpec((None, D_MODEL, ff), lambda i, e: (e, 0, 0)),
                  pl.BlockSpec((None, ff, D_MODEL), lambda i, e: (e, 0, 0)),
                  pl.BlockSpec((1, D_MODEL), c2)],
        out_specs=pl.BlockSpec((tm, D_MODEL), lambda i, e: (i, 0)),
        out_shape=jax.ShapeDtypeStruct((m, D_MODEL), F32),
        scratch_shapes=[pltpu.VMEM((tm, D_MODEL), BF16), pltpu.VMEM((tm, D_MODEL), F32),
                        pltpu.VMEM((tm, LANES), F32)],
        compiler_params=_cparams("parallel", "arbitrary"),
    )(x, g, rw_hi, rw_lo, rb, wg, wu, wd, fg)


def _hgrn_step_kernel(p_ref, lb_ref, g_ref, s_ref, o_ref, s_out_ref, qt_ref, kt_ref, dt_ref, vt_ref, ot_ref):
    h = pl.program_id(0)

    @pl.when(h == 0)
    def _():
        logf, kk, q = _hgrn_gates(p_ref[:, OFF_HQ:OFF_HQ + SEG], p_ref[:, OFF_HF:OFF_HF + SEG], lb_ref[...])
        qt_ref[...] = q.T
        kt_ref[...] = kk.T
        dt_ref[...] = jnp.exp(logf).T
        vt_ref[...] = p_ref[:, OFF_HI:OFF_HI + SEG].T
        ot_ref[...] = jnp.zeros_like(ot_ref)

    base = pl.multiple_of(h * HEAD, HEAD)
    vh = vt_ref[pl.ds(base, HEAD), :]

    def body(k, o):
        s_new = dt_ref[pl.ds(base + k, 1), :] * s_ref[k] + kt_ref[pl.ds(base + k, 1), :] * vh
        s_out_ref[k] = s_new
        return o + qt_ref[pl.ds(base + k, 1), :] * s_new

    ot_ref[pl.ds(base, HEAD), :] = lax.fori_loop(0, HEAD, body, jnp.zeros(vh.shape, F32), unroll=8)

    @pl.when(h == N_HEADS - 1)
    def _():
        hg = p_ref[:, OFF_HG:OFF_HG + SEG]
        o_ref[...] = _head_rms(ot_ref[...].T, g_ref[...]) * (hg * _sigmoid(hg))


def hgrn_step(proj, lb, g, state, li):
    n = proj.shape[0]
    c2 = lambda h: (0, 0)
    st = pl.BlockSpec((None, None, HEAD, HEAD, n), lambda h: (li, h, 0, 0, 0))
    return pl.pallas_call(
        _hgrn_step_kernel,
        grid=(N_HEADS,),
        in_specs=[pl.BlockSpec((n, 4 * SEG), c2), pl.BlockSpec((1, SEG), c2), pl.BlockSpec((1, SEG), c2), st],
        out_specs=[pl.BlockSpec((n, SEG), c2),
                   pl.BlockSpec((None, HEAD, HEAD, n), lambda h: (h, 0, 0, 0))],
        out_shape=[jax.ShapeDtypeStruct((n, SEG), F32), jax.ShapeDtypeStruct((N_HEADS, HEAD, HEAD, n), F32)],
        scratch_shapes=[pltpu.VMEM((SEG, n), F32)] * 5,
        compiler_params=_cparams("arbitrary"),
    )(proj, lb, g, state)


def _mlstm_step_kernel(mu_ref, mv_ref, mo_ref, gate_ref, conv_ref, cw_ref, cb_ref, wq_ref, wk_ref, gb_ref,
                       g_ref, c_ref, n_ref, m_ref,
                       o_ref, c_out_ref, n_out_ref, m_out_ref, conv_out_ref,
                       qt_ref, kt_ref, vt_ref, gt_ref, ht_ref, kw_ref):
    h = pl.program_id(0)
    n5 = N_HEADS * HEAD

    @pl.when(h == 0)
    def _():
        mut = mu_ref[...].T[:n5]
        y = cb_ref[...] + cw_ref[CONV_W - 1] * mut
        for j in range(CONV_W - 1):
            y = y + cw_ref[j] * conv_ref[j]
            if j > 0:
                conv_out_ref[j - 1] = conv_ref[j]
        conv_out_ref[CONV_W - 2] = mut
        cact = jnp.concatenate([y * _sigmoid(y), jnp.zeros((SEG - n5, y.shape[1]), F32)], axis=0).astype(BF16)
        for p in range(N_PAIRS):
            sl = slice(p * LANES, (p + 1) * LANES)
            qt_ref[sl, :] = _dot(wq_ref[p], cact[sl])
            kt_ref[sl, :] = _dot(wk_ref[p], cact[sl])
        vt_ref[...] = mv_ref[...].T
        gt_ref[...] = gate_ref[...].T + gb_ref[...]
        ht_ref[...] = jnp.zeros_like(ht_ref)
        m_out_ref[...] = jnp.zeros_like(m_out_ref)

    ig = gt_ref[pl.ds(h, 1), :]
    log_int = _log_sigmoid(gt_ref[pl.ds(h + N_HEADS, 1), :]) + m_ref[pl.ds(h, 1), :]
    mt = jnp.maximum(log_int, ig)
    dw = jnp.exp(ig - mt)
    ai = jnp.exp(log_int - mt)
    m_out_ref[pl.ds(h, 1), :] = mt

    base = pl.multiple_of(h * HEAD, HEAD)
    qh = qt_ref[pl.ds(base, HEAD), :]
    vh = vt_ref[pl.ds(base, HEAD), :]
    kw = kt_ref[pl.ds(base, HEAD), :] * dw
    kw_ref[...] = kw
    n_new = ai * n_ref[...] + kw
    n_out_ref[...] = n_new
    nq = jnp.sum(qh * n_new, axis=0, keepdims=True)

    def body(k, num):
        c_new = ai * c_ref[k] + kw_ref[pl.ds(k, 1), :] * vh
        c_out_ref[k] = c_new
        return num + qt_ref[pl.ds(base + k, 1), :] * c_new

    num = lax.fori_loop(0, HEAD, body, jnp.zeros(vh.shape, F32), unroll=8)
    hh = num / jnp.maximum(jnp.abs(nq), jnp.exp(-mt))
    ht_ref[pl.ds(base, HEAD), :] = hh * lax.rsqrt(jnp.mean(hh * hh, axis=0, keepdims=True) + EPS)

    @pl.when(h == N_HEADS - 1)
    def _():
        o_ref[...] = ht_ref[...].T * g_ref[...] * _sigmoid(mo_ref[...])


def mlstm_step(proj, conv, cw, cb, wq_t, wk_t, gb, g, c_state, n_state, m_state, li):
    n = proj.shape[0]
    n5 = N_HEADS * HEAD
    c2 = lambda h: (0, 0)
    c3 = lambda h: (0, 0, 0)
    return pl.pallas_call(
        _mlstm_step_kernel,
        grid=(N_HEADS,),
        in_specs=[pl.BlockSpec((n, SEG), lambda h: (0, OFF_MU // SEG)),
                  pl.BlockSpec((n, SEG), lambda h: (0, OFF_MV // SEG)),
                  pl.BlockSpec((n, SEG), lambda h: (0, OFF_MO // SEG)),
                  pl.BlockSpec((n, LANES), lambda h: (0, OFF_GATE // LANES)),
                  pl.BlockSpec((None, CONV_W - 1, n5, n), lambda h: (li, 0, 0, 0)),
                  pl.BlockSpec((CONV_W, n5, n), c3), pl.BlockSpec((n5, n), c2),
                  pl.BlockSpec((N_PAIRS, LANES, LANES), c3), pl.BlockSpec((N_PAIRS, LANES, LANES), c3),
                  pl.BlockSpec((LANES, n), c2), pl.BlockSpec((1, SEG), c2),
                  pl.BlockSpec((None, None, HEAD, HEAD, n), lambda h: (li, h, 0, 0, 0)),
                  pl.BlockSpec((None, None, HEAD, n), lambda h: (li, h, 0, 0)),
                  pl.BlockSpec((8, n), c2)],
        out_specs=[pl.BlockSpec((n, SEG), c2),
                   pl.BlockSpec((None, HEAD, HEAD, n), lambda h: (h, 0, 0, 0)),
                   pl.BlockSpec((None, HEAD, n), lambda h: (h, 0, 0)),
                   pl.BlockSpec((8, n), c2),
                   pl.BlockSpec((CONV_W - 1, n5, n), c3)],
        out_shape=[jax.ShapeDtypeStruct((n, SEG), F32),
                   jax.ShapeDtypeStruct((N_HEADS, HEAD, HEAD, n), F32),
                   jax.ShapeDtypeStruct((N_HEADS, HEAD, n), F32),
                   jax.ShapeDtypeStruct((8, n), F32),
                   jax.ShapeDtypeStruct((CONV_W - 1, n5, n), F32)],
        scratch_shapes=[pltpu.VMEM((SEG, n), F32), pltpu.VMEM((SEG, n), F32), pltpu.VMEM((SEG, n), F32),
                        pltpu.VMEM((LANES, n), F32), pltpu.VMEM((SEG, n), F32), pltpu.VMEM((HEAD, n), F32)],
        compiler_params=_cparams("arbitrary"),
    )(proj, proj, proj, proj, conv, cw, cb, wq_t, wk_t, gb, g, c_state, n_state, m_state)


def _lane_tile(x, reps):
    return x if reps == 1 else jnp.concatenate([x] * reps, axis=1)


def _swa_step_kernel(*refs, nb, aliased):
    q_ref, k_ref, v_ref, c0_ref, c1_ref, c2_ref = refs[:6]
    o_ref, n0_ref, n1_ref, n2_ref, bt_ref, ot_ref = refs[-6:]
    i = pl.program_id(0)
    n = q_ref.shape[0]

    @pl.when(i == 0)
    def _():
        bt_ref[0:SEG, :] = (q_ref[...] * (HEAD ** -0.5)).T
        bt_ref[SEG:2 * SEG, :] = k_ref[...].T
        bt_ref[2 * SEG:3 * SEG, :] = v_ref[...].T
        ot_ref[...] = jnp.zeros_like(ot_ref)

    lane_n = _iota((HEAD, n), 1)
    for t in range(nb):
        tok = i * nb + t
        onehot = jnp.where(_iota((n, LANES), 0) == tok, 1.0, 0.0).astype(BF16)
        bc = _dot_sel(bt_ref[...], onehot)
        outs, lses = [], []
        for g, (c_ref, n_ref) in enumerate(((c0_ref, n0_ref), (c1_ref, n1_ref), (c2_ref, n2_ref))):
            w = c_ref.shape[-1]
            dil = SWA_PAIRS[g][1]
            reps = w // LANES
            live = _iota((1, w), 1) % dil == 0
            last = _iota((HEAD, w), 1) == w - 1
            for j in range(2):
                r = g * LANES + j * HEAD
                qb = bc[r:r + HEAD]
                kb = bc[SEG + r:SEG + r + HEAD]
                vb = bc[2 * SEG + r:2 * SEG + r + HEAD]
                kc = c_ref[t, 0, j]
                vc = c_ref[t, 1, j]
                s = jnp.where(live, jnp.sum(_lane_tile(qb, reps) * kc, axis=0, keepdims=True), -jnp.inf)
                sn = jnp.sum(qb * kb, axis=0, keepdims=True)
                mx = jnp.maximum(jnp.max(s, axis=1, keepdims=True), sn)
                p = jnp.exp(s - _lane_tile(mx, reps))
                pn = jnp.exp(sn - mx)
                den = jnp.sum(p, axis=1, keepdims=True) + pn
                outs.append((jnp.sum(p * vc, axis=1, keepdims=True) + pn * vb) / den)
                lses.append(mx + jnp.log(den))
                n_ref[t, 0, j] = jnp.where(last, _lane_tile(kb, reps), pltpu.roll(kc, w - 1, axis=1))
                n_ref[t, 1, j] = jnp.where(last, _lane_tile(vb, reps), pltpu.roll(vc, w - 1, axis=1))
        for j in range(2):
            ls = [lses[2 * g + j] for g in range(N_GROUPS)]
            mx = jnp.maximum(jnp.maximum(ls[0], ls[1]), ls[2])
            es = [jnp.exp(l - mx) for l in ls]
            inv = 1.0 / (es[0] + es[1] + es[2])
            for g in range(N_GROUPS):
                r = g * LANES + j * HEAD
                col = outs[2 * g + j] * (es[g] * inv)
                col = col if n == LANES else col[:, :n]
                ot_ref[r:r + HEAD, :] = jnp.where(lane_n == tok, col, ot_ref[r:r + HEAD, :])

    @pl.when(i == pl.num_programs(0) - 1)
    def _():
        o_ref[...] = ot_ref[...].T


def swa_step(proj, caches, prev, li, nb):
    n = proj.shape[0]
    aliased = prev is not None
    c2 = lambda i: (0, 0)

    def cspec(c):
        return pl.BlockSpec((None, nb, 2, 2, HEAD, c.shape[-1]), lambda i: (li, i, 0, 0, 0, 0))

    in_specs = [pl.BlockSpec((n, SEG), lambda i: (0, OFF_SQ // SEG)),
                pl.BlockSpec((n, SEG), lambda i: (0, OFF_SK // SEG)),
                pl.BlockSpec((n, SEG), lambda i: (0, OFF_SV // SEG))] + [cspec(c) for c in caches]
    args = [proj, proj, proj] + list(caches)
    aliases = {}
    if aliased:
        in_specs += [pl.BlockSpec(memory_space=pl.ANY)] * N_GROUPS
        aliases = {len(args) + g: 1 + g for g in range(N_GROUPS)}
        args += list(prev)
    res = pl.pallas_call(
        functools.partial(_swa_step_kernel, nb=nb, aliased=aliased),
        grid=(n // nb,),
        in_specs=in_specs,
        out_specs=[pl.BlockSpec((n, SEG), c2)] + [cspec(c) for c in caches],
        out_shape=[jax.ShapeDtypeStruct((n, SEG), F32)] + [jax.ShapeDtypeStruct(c.shape, F32) for c in caches],
        scratch_shapes=[pltpu.VMEM((3 * SEG, n), F32), pltpu.VMEM((SEG, n), F32)],
        input_output_aliases=aliases,
        compiler_params=_cparams("arbitrary"),
    )(*args)
    return res[0], res[1:]


def _mix_out_step_kernel(oh_ref, om_ref, os_ref, x_ref, w_ref, y_ref):
    mix = jnp.concatenate([oh_ref[...], om_ref[...], os_ref[...]], axis=1)
    y_ref[...] = x_ref[...] + _dot(mix.astype(BF16), w_ref[...])


def mix_out_step(oh, om, os_, x, w):
    m = x.shape[0]
    c2 = lambda i: (0, 0)
    seg = pl.BlockSpec((m, SEG), c2)
    return pl.pallas_call(
        _mix_out_step_kernel,
        grid=(1,),
        in_specs=[seg, seg, seg, pl.BlockSpec((m, D_MODEL), c2), pl.BlockSpec((3 * SEG, D_MODEL), c2)],
        out_specs=pl.BlockSpec((m, D_MODEL), c2),
        out_shape=jax.ShapeDtypeStruct((m, D_MODEL), F32),
        compiler_params=_cparams("arbitrary"),
    )(oh, om, os_, x, w)


def _pad_cols(a, n):
    return jnp.pad(a, [(0, 0)] * (a.ndim - 1) + [(0, n - a.shape[-1])])


def _relayout_w_in(w):
    segs = jnp.split(w, np.cumsum(IN_SPLITS)[:-1].tolist(), axis=1)
    hq, hf, hi, hg, mu, mv, mo, mig, mfg, sq, sk, sv = segs
    cols = [_pad_cols(s, SEG) for s in (hq, hf, hi, hg, mu, mv, mo)] + [sq, sk, sv]
    cols.append(_pad_cols(jnp.concatenate([mig, mfg], axis=1), LANES))
    return jnp.concatenate(cols, axis=1).astype(BF16)


def _relayout_w_out(w):
    n5 = N_HEADS * HEAD
    rows = [w[:n5], jnp.zeros((SEG - n5, D_MODEL), w.dtype), w[n5:2 * n5], jnp.zeros((SEG - n5, D_MODEL), w.dtype),
            w[2 * n5:]]
    return jnp.concatenate(rows, axis=0).astype(BF16)


def _pair_block_diag(w):
    w6 = jnp.concatenate([w, jnp.zeros((1, HEAD, HEAD), w.dtype)], axis=0)
    z = jnp.zeros((HEAD, HEAD), w.dtype)
    return jnp.stack([jnp.block([[w6[2 * p], z], [z, w6[2 * p + 1]]]) for p in range(N_PAIRS)], axis=0)


def _row(a, n):
    return _pad_cols(a.reshape(1, -1).astype(F32), n)


def _unpair_state(st):
    heads = []
    for h in range(N_HEADS):
        p, j = divmod(h, 2)
        heads.append(st[:, p, j * HEAD:(j + 1) * HEAD, j * HEAD:(j + 1) * HEAD])
    return jnp.stack(heads, axis=1)


def kernel(x_prompt, x_sample, state_hgrn, state_mlstm_C, state_mlstm_n, state_mlstm_m, state_mlstm_conv,
           cache_swa_w128, cache_swa_w512, cache_swa_w2048, norm1_g, norm2_g, final_norm_g, w_in, w_out,
           hgrn_lb_param, hgrn_onorm_g, mlstm_conv_w, mlstm_conv_b, mlstm_wq, mlstm_wk, mlstm_ig_b,
           mlstm_fg_b, mlstm_onorm_g, ffn_w_gate, ffn_w_up, ffn_w_down, moe_router_w, moe_router_b,
           moe_w_gate, moe_w_up, moe_w_down):
    batch, seq, _ = x_prompt.shape
    n_dec = x_sample.shape[0]
    depth = w_in.shape[0]
    caches = (cache_swa_w128, cache_swa_w512, cache_swa_w2048)

    p_lb = jax.nn.softmax(hgrn_lb_param.astype(F32), axis=0)
    cs = jnp.cumsum(p_lb, axis=0)
    lb_all = cs - cs[0:1]

    xp = x_prompt.reshape(batch * seq, D_MODEL)
    xs = x_sample.reshape(n_dec, D_MODEL)
    out_p = {k: [] for k in ("hgrn", "C", "n", "m", "conv", "swa0", "swa1", "swa2")}
    out_s = {k: [] for k in ("hgrn", "C", "n", "m", "conv")}

    n5 = N_HEADS * HEAD
    hgrn_t = state_hgrn.astype(F32).transpose(0, 2, 3, 4, 1)
    c_t = state_mlstm_C.astype(F32).transpose(0, 2, 3, 4, 1)
    n_t = state_mlstm_n.astype(F32).transpose(0, 2, 3, 1)
    conv_t = state_mlstm_conv.astype(F32).transpose(0, 2, 3, 1)
    caches_t = [c.astype(F32).transpose(0, 1, 3, 4, 5, 2) for c in caches]
    new_caches = None

    for li in range(depth):
        w_in_l = _relayout_w_in(w_in[li])
        w_out_l = _relayout_w_out(w_out[li])
        n1 = norm1_g[li].reshape(1, D_MODEL)
        n2 = norm2_g[li].reshape(1, D_MODEL)
        lb = jnp.pad(lb_all[li].reshape(1, -1), ((0, 0), (0, SEG - N_HEADS * HEAD)), constant_values=0.5)
        hg_g = _row(hgrn_onorm_g[li], SEG)
        ml_g = _row(mlstm_onorm_g[li], SEG)
        cw = jnp.pad(mlstm_conv_w[li].astype(F32), ((0, 8 - CONV_W), (0, SEG - N_HEADS * HEAD)))
        cb = _row(mlstm_conv_b[li], SEG)
        wq = _pair_block_diag(mlstm_wq[li]).astype(BF16)
        wk = (_pair_block_diag(mlstm_wk[li]) * (HEAD ** -0.5)).astype(BF16)
        gb = _row(jnp.concatenate([mlstm_ig_b[li], mlstm_fg_b[li]]), LANES)

        proj = norm_proj(xp, n1, w_in_l, 256)
        oh, s_h = hgrn_prompt(proj, lb, hg_g, batch, seq)
        om, s_c, s_n, s_m = mlstm_prompt(proj, cw, cb, wq, wk, gb, ml_g, batch, seq)
        p3 = proj.reshape(batch, seq, N_PROJ)
        os_, ls_ = [], []
        for g, (win, dil) in enumerate(SWA_PAIRS):
            def strided(off):
                a = p3[:, :, off + g * LANES:off + (g + 1) * LANES]
                return a.reshape(batch, seq // dil, dil, LANES).transpose(0, 2, 1, 3)
            o, lse = swa_prompt(strided(OFF_SQ), strided(OFF_SK), strided(OFF_SV))
            os_.append(o.transpose(0, 2, 1, 3).reshape(batch * seq, LANES))
            ls_.append(lse.transpose(0, 2, 1, 3).reshape(batch * seq, LANES))
            keep = min(win, seq)
            kv = jnp.stack([p3[:, seq - keep:, OFF_SK + g * LANES:OFF_SK + (g + 1) * LANES],
                            p3[:, seq - keep:, OFF_SV + g * LANES:OFF_SV + (g + 1) * LANES]], axis=2)
            out_p["swa%d" % g].append(kv.reshape(batch, keep, 2, 2, HEAD))
        xp = mix_out(oh, om, os_, ls_, xp, w_out_l, 512)
        out_p["hgrn"].append(_unpair_state(s_h))
        out_p["C"].append(_unpair_state(s_c))
        out_p["n"].append(jnp.stack([s_n[:, h // 2, (h % 2) * HEAD:(h % 2 + 1) * HEAD, (h % 2) * HEAD]
                                     for h in range(N_HEADS)], axis=1))
        out_p["m"].append(s_m[:, 0, 0:N_HEADS * HEAD:HEAD])
        out_p["conv"].append(p3[:, seq - (CONV_W - 1):, OFF_MU:OFF_MU + N_HEADS * HEAD])

        projs = norm_proj(xs, n1, w_in_l, n_dec)
        ohs, hs_new = hgrn_step(projs, lb, hg_g, hgrn_t, li)
        cw_t = jnp.broadcast_to(mlstm_conv_w[li].astype(F32)[:, :, None], (CONV_W, n5, n_dec))
        cb_t = jnp.broadcast_to(mlstm_conv_b[li].astype(F32)[:, None], (n5, n_dec))
        gb_t = jnp.broadcast_to(_pad_cols(gb, LANES).reshape(LANES, 1), (LANES, n_dec))
        m_in = jnp.pad(state_mlstm_m[li].astype(F32).T, ((0, 8 - N_HEADS), (0, 0)))
        oms, c_new, n_new, m_new, conv_new = mlstm_step(
            projs, conv_t, cw_t, cb_t, wq.transpose(0, 2, 1), wk.transpose(0, 2, 1), gb_t, ml_g,
            c_t, n_t, m_in, li)
        oss, new_caches = swa_step(projs, caches_t, new_caches, li, 2)
        xs = mix_out_step(ohs, oms, oss, xs, w_out_l)
        out_s["hgrn"].append(hs_new)
        out_s["C"].append(c_new)
        out_s["n"].append(n_new)
        out_s["m"].append(m_new[:N_HEADS])
        out_s["conv"].append(conv_new)

        final = li == depth - 1
        fg = final_norm_g.reshape(1, D_MODEL)
        j = li // 2
        if li % 2 == 0:
            wg, wu, wd = (ffn_w_gate[j].astype(BF16), ffn_w_up[j].astype(BF16), ffn_w_down[j].astype(BF16))
            xp = ffn_dense(xp, n2, wg, wu, wd, fg, final, 512, 1408)
            xs = ffn_dense(xs, n2, wg, wu, wd, fg, final, n_dec, 1408)
        else:
            rw = _pad_cols(moe_router_w[j].astype(F32), LANES)
            rw_hi = rw.astype(BF16)
            rw_lo = (rw - rw_hi.astype(F32)).astype(BF16)
            rb = _row(moe_router_b[j], LANES)
            wg, wu, wd = (moe_w_gate[j].astype(BF16), moe_w_up[j].astype(BF16), moe_w_down[j].astype(BF16))
            xp = ffn_moe(xp, n2, rw_hi, rw_lo, rb, wg, wu, wd, fg, final, 512)
            xs = ffn_moe(xs, n2, rw_hi, rw_lo, rb, wg, wu, wd, fg, final, n_dec)

    stk = lambda lst: jnp.stack(lst, axis=0)
    cache_out = [c.transpose(0, 1, 5, 2, 3, 4) for c in new_caches]
    return (xp.reshape(batch, seq, D_MODEL), xs.reshape(n_dec, 1, D_MODEL),
            stk(out_p["hgrn"]), stk(out_s["hgrn"]).transpose(0, 4, 1, 2, 3),
            stk(out_p["C"]), stk(out_s["C"]).transpose(0, 4, 1, 2, 3),
            stk(out_p["n"]), stk(out_s["n"]).transpose(0, 3, 1, 2),
            stk(out_p["m"]), stk(out_s["m"]).transpose(0, 2, 1),
            stk(out_p["conv"]), stk(out_s["conv"]).transpose(0, 3, 1, 2),
            stk(out_p["swa0"]), cache_out[0],
            stk(out_p["swa1"]), cache_out[1],
            stk(out_p["swa2"]), cache_out[2])
```

```python
import functools

import numpy as np
import jax
import jax.numpy as jnp
from jax import lax
from jax.experimental import pallas as pl
from jax.experimental.pallas import tpu as pltpu

F32 = jnp.float32
BF16 = jnp.bfloat16

D_MODEL = 1024
HEAD = 64
N_HEADS = 5
N_PAIRS = 3
SEG = 384
LANES = 128
FLAT = HEAD * HEAD
CONV_W = 4
SWA_PAIRS = ((128, 1), (512, 4), (2048, 16))
SWA_J = 128
N_GROUPS = 3
EPS = 1e-6
N_EXPERTS = 8
IN_SPLITS = (320, 320, 320, 320, 320, 320, 320, 5, 5, 384, 384, 384)

OFF_HQ, OFF_HF, OFF_HI, OFF_HG = 0, 384, 768, 1152
OFF_MU, OFF_MV, OFF_MO = 1536, 1920, 2304
OFF_SQ, OFF_SK, OFF_SV = 2688, 3072, 3456
OFF_GATE = 3840
N_PROJ = 3968

CHUNK = 128
MOE_SUB = 512
MOE_CHUNK = 160
VMEM_LIMIT = 56 * 1024 * 1024


def _cparams(*sem):
    return pltpu.CompilerParams(dimension_semantics=sem, vmem_limit_bytes=VMEM_LIMIT)


def _dot(a, b):
    return jnp.dot(a, b, preferred_element_type=F32)


def _dot_nt(a, b):
    return lax.dot_general(a, b, (((1,), (1,)), ((), ())), preferred_element_type=F32)


def _split(x, n):
    parts = []
    r = x
    for i in range(n):
        p = r.astype(BF16)
        parts.append(p)
        if i + 1 < n:
            r = r - p.astype(F32)
    return parts


def _dot_sel(x, sel, n=3):
    acc = None
    for p in _split(x, n):
        t = _dot(p, sel)
        acc = t if acc is None else acc + t
    return acc


def _sel_dot(sel, x, n=3):
    acc = None
    for p in _split(x, n):
        t = _dot(sel, p)
        acc = t if acc is None else acc + t
    return acc


def _sel_dot_nt(sel, x, n=3):
    acc = None
    for p in _split(x, n):
        t = _dot_nt(sel, p)
        acc = t if acc is None else acc + t
    return acc


def _sigmoid(x):
    return 1.0 / (1.0 + jnp.exp(-x))


def _log_sigmoid(x):
    return jnp.minimum(x, 0.0) - jnp.log(1.0 + jnp.exp(-jnp.abs(x)))


def _iota(shape, axis):
    return lax.broadcasted_iota(jnp.int32, shape, axis)


def _tril_bf16(n):
    return jnp.where(_iota((n, n), 0) >= _iota((n, n), 1), 1.0, 0.0).astype(BF16)


def _same_head(n_rows, n_cols):
    return (_iota((n_rows, n_cols), 0) // HEAD) == (_iota((n_rows, n_cols), 1) // HEAD)


def _head_ones():
    return jnp.where(_same_head(LANES, LANES), 1.0, 0.0).astype(BF16)


def _stack_heads(x):
    lane = _iota(x.shape, 1)
    zero = jnp.zeros_like(x)
    return jnp.concatenate([jnp.where(lane < HEAD, x, zero).astype(BF16),
                            jnp.where(lane >= HEAD, x, zero).astype(BF16)], axis=0)


def _stack_heads_ones(x):
    lane = _iota(x.shape, 1)
    zero = jnp.zeros_like(x)
    one = jnp.ones_like(x)
    top = jnp.concatenate([jnp.where(lane < HEAD, x, zero), jnp.where(lane < HEAD, one, zero)], axis=1)
    bot = jnp.concatenate([jnp.where(lane >= HEAD, x, zero), jnp.where(lane >= HEAD, one, zero)], axis=1)
    return jnp.concatenate([top.astype(BF16), bot.astype(BF16)], axis=0)


def _pair_sums(x, n=2):
    ones = _head_ones()
    cols = [_dot_sel(x[:, p * LANES:(p + 1) * LANES], ones, n) for p in range(x.shape[1] // LANES)]
    return cols[0] if len(cols) == 1 else jnp.concatenate(cols, axis=1)


def _head_rms(o, g):
    msq = _pair_sums(o * o) * (1.0 / HEAD)
    return o * lax.rsqrt(msq + EPS) * g


def _rmsnorm(x, g):
    return x * lax.rsqrt(jnp.mean(x * x, axis=-1, keepdims=True) + EPS) * g


def _norm_proj_kernel(x_ref, g_ref, w_ref, o_ref):
    o_ref[...] = _dot_nt(_rmsnorm(x_ref[...], g_ref[...]).astype(BF16), w_ref[...])


def norm_proj(x, g, w, tm):
    m = x.shape[0]
    n = w.shape[0]
    return pl.pallas_call(
        _norm_proj_kernel,
        grid=(m // tm,),
        in_specs=[pl.BlockSpec((tm, D_MODEL), lambda i: (i, 0)),
                  pl.BlockSpec((1, D_MODEL), lambda i: (0, 0)),
                  pl.BlockSpec((n, D_MODEL), lambda i: (0, 0))],
        out_specs=pl.BlockSpec((tm, n), lambda i: (i, 0)),
        out_shape=jax.ShapeDtypeStruct((m, n), F32),
        compiler_params=_cparams("parallel"),
    )(x, g, w)


def _hgrn_level_map(L):
    t = np.arange(L)[:, None]
    s = np.arange(L)[None, :]
    lev = np.full((L, L), -1, np.int32)
    n_lev = int(np.log2(L))
    for k in range(n_lev):
        h = L >> (k + 1)
        ok = (t // (2 * h) == s // (2 * h)) & (t % (2 * h) >= h) & (s % (2 * h) < h)
        lev[ok] = k
    lev[t == s] = n_lev
    return np.concatenate([lev, lev], axis=0)


def _block_ref(b, h):
    L = b.shape[0]
    if 2 * h >= 8:
        pieces = []
        for j in range(L // (2 * h)):
            r0 = j * 2 * h + h - 1
            pieces.append(jnp.broadcast_to(b[r0:r0 + 1, :], (2 * h, b.shape[1])))
        return jnp.concatenate(pieces, axis=0)
    pos = _iota(b.shape, 0) % (2 * h)
    out = b
    for d in range(-(h - 1), h + 1):
        if d != 0:
            out = jnp.where(pos == h - 1 + d, pltpu.roll(b, d % L, axis=0), out)
    return out


def _hgrn_gates(hq, z, lb):
    la = jnp.log(lb)
    lc = jnp.log1p(-lb) + _log_sigmoid(z)
    logf = jnp.maximum(la, lc) + jnp.log(1.0 + jnp.exp(-jnp.abs(la - lc)))
    kk = (1.0 - lb) * _sigmoid(-z)
    q = hq * _sigmoid(hq)
    return logf, kk, q


def _hgrn_kernel(p_ref, lb_ref, g_ref, lev_ref, o_ref, s_out_ref, s_ref):
    c = pl.program_id(1)
    L = p_ref.shape[0]

    @pl.when(c == 0)
    def _():
        s_ref[...] = jnp.zeros_like(s_ref)

    hq = p_ref[:, OFF_HQ:OFF_HQ + SEG]
    z = p_ref[:, OFF_HF:OFF_HF + SEG]
    v = p_ref[:, OFF_HI:OFF_HI + SEG]
    hg = p_ref[:, OFF_HG:OFF_HG + SEG]
    logf, kk, q = _hgrn_gates(hq, z, lb_ref[...])

    b = _sel_dot(_tril_bf16(L), logf)
    b_last = b[L - 1:L, :]

    lev = lev_ref[...]
    n_lev = int(np.log2(L))
    row = _iota((L, SEG), 0)
    acc = [jnp.zeros((2 * L, L), F32) for _ in range(N_PAIRS)]
    for k in range(n_lev + 1):
        if k < n_lev:
            h = L >> (k + 1)
            ex = jnp.exp(-jnp.abs(b - _block_ref(b, h)))
            xl = jnp.where(row % (2 * h) >= h, q, kk) * ex
            xr = xl
        else:
            xl = q
            xr = kk
        for p in range(N_PAIRS):
            sl = slice(p * LANES, (p + 1) * LANES)
            sc = _dot_nt(_stack_heads(xl[:, sl]), xr[:, sl].astype(BF16))
            acc[p] = jnp.where(lev == k, sc, acc[p])

    qd = (q * jnp.exp(b)).astype(BF16)
    kd = kk * jnp.exp(b_last - b)
    vb = v.astype(BF16)
    same = _same_head(LANES, LANES)
    outs = []
    for p in range(N_PAIRS):
        sl = slice(p * LANES, (p + 1) * LANES)
        a2 = jnp.concatenate([acc[p][:L], acc[p][L:]], axis=1).astype(BF16)
        s_old = s_ref[p]
        outs.append(_dot(a2, _stack_heads(v[:, sl])) + _dot(qd[:, sl], s_old.astype(BF16)))
        dec = jnp.broadcast_to(jnp.exp(b_last[:, sl]), (LANES, LANES)).T
        kv = _dot(kd[:, sl].T.astype(BF16), vb[:, sl])
        s_ref[p] = dec * s_old + jnp.where(same, kv, 0.0)
    o = jnp.concatenate(outs, axis=1)
    o_ref[...] = _head_rms(o, g_ref[...]) * (hg * _sigmoid(hg))

    @pl.when(c == pl.num_programs(1) - 1)
    def _():
        s_out_ref[...] = s_ref[...]


def hgrn_prompt(proj, lb, g, batch, seq):
    L = CHUNK
    nc = seq // L
    lev = jnp.asarray(_hgrn_level_map(L))
    return pl.pallas_call(
        _hgrn_kernel,
        grid=(batch, nc),
        in_specs=[pl.BlockSpec((L, 4 * SEG), lambda b, c: (b * nc + c, 0)),
                  pl.BlockSpec((1, SEG), lambda b, c: (0, 0)),
                  pl.BlockSpec((1, SEG), lambda b, c: (0, 0)),
                  pl.BlockSpec((2 * L, L), lambda b, c: (0, 0))],
        out_specs=[pl.BlockSpec((L, SEG), lambda b, c: (b * nc + c, 0)),
                   pl.BlockSpec((None, N_PAIRS, LANES, LANES), lambda b, c: (b, 0, 0, 0))],
        out_shape=[jax.ShapeDtypeStruct((batch * seq, SEG), F32),
                   jax.ShapeDtypeStruct((batch, N_PAIRS, LANES, LANES), F32)],
        scratch_shapes=[pltpu.VMEM((N_PAIRS, LANES, LANES), F32)],
        compiler_params=_cparams("parallel", "arbitrary"),
    )(proj, lb, g, lev)


def _gate_selectors():
    x_ig = np.zeros((LANES, SEG), np.float32)
    x_f = np.zeros((LANES, SEG), np.float32)
    x_ft = np.zeros((LANES, N_HEADS * LANES), np.float32)
    sel_d = np.zeros((8, LANES), np.float32)
    for h in range(N_HEADS):
        x_ig[h, h * HEAD:(h + 1) * HEAD] = 1.0
        x_f[N_HEADS + h, h * HEAD:(h + 1) * HEAD] = 1.0
        x_ft[N_HEADS + h, h * LANES:(h + 1) * LANES] = 1.0
        sel_d[h, h] = 1.0
        sel_d[h, N_HEADS + h] = -1.0
    return (jnp.asarray(x_ig, BF16), jnp.asarray(x_f, BF16), jnp.asarray(x_ft, BF16),
            jnp.asarray(sel_d, BF16))


def _mlstm_kernel(mu_ref, mv_ref, mo_ref, gate_ref, cw_ref, cb_ref, wq_ref, wk_ref, gb_ref, g_ref,
                  xig_ref, xf_ref, xft_ref, seld_ref,
                  o_ref, c_out_ref, n_out_ref, m_out_ref,
                  u_ref, c_ref, n_ref, m1_ref, m2_ref):
    ci = pl.program_id(1)
    L = mu_ref.shape[0]

    @pl.when(ci == 0)
    def _():
        u_ref[0:8, :] = jnp.zeros((8, SEG), F32)
        c_ref[...] = jnp.zeros_like(c_ref)
        n_ref[...] = jnp.zeros_like(n_ref)
        m1_ref[...] = jnp.zeros_like(m1_ref)
        m2_ref[...] = jnp.zeros_like(m2_ref)

    @pl.when(ci > 0)
    def _():
        u_ref[0:8, :] = u_ref[L:L + 8, :]

    u_ref[8:L + 8, :] = mu_ref[...]

    y = cb_ref[...] + cw_ref[CONV_W - 1:CONV_W, :] * u_ref[8:L + 8, :]
    for j in range(1, CONV_W):
        y = y + cw_ref[CONV_W - 1 - j:CONV_W - j, :] * u_ref[8 - j:8 - j + L, :]
    cact = (y * _sigmoid(y)).astype(BF16)
    v = mv_ref[...]
    vb = v.astype(BF16)

    gfull = gate_ref[...] + gb_ref[...]
    lane = _iota((L, LANES), 1)
    fcum = _sel_dot(_tril_bf16(L), _log_sigmoid(gfull))
    gv = jnp.where(lane < N_HEADS, gfull, fcum)
    igx = _dot_sel(gv, xig_ref[...])
    fx = _dot_sel(gv, xf_ref[...])
    ftile = _dot_sel(gv, xft_ref[...])
    rowv = _sel_dot_nt(seld_ref[...], gv)
    m1 = m1_ref[...]
    m2 = m2_ref[...]

    causal = _iota((L, L), 0) >= _iota((L, L), 1)
    lane_p = _iota((L, LANES), 1)
    same = _same_head(LANES, LANES)
    ones_l = jnp.ones((L, LANES), BF16)

    dws, mtt = [], []
    for h in range(N_HEADS):
        ft = ftile[:, h * LANES:(h + 1) * LANES]
        logd = jnp.where(causal, ft + rowv[h:h + 1, :], -jnp.inf)
        mt = jnp.maximum(ft + m2[:, h * LANES:(h + 1) * LANES], jnp.max(logd, axis=-1, keepdims=True))
        dws.append(jnp.exp(logd - mt))
        mtt.append(mt)
    m2_ref[...] = jnp.concatenate([mt[L - 1:L, :] for mt in mtt], axis=1)
    zeros_t = jnp.zeros((L, LANES), F32)
    mtx = jnp.concatenate([jnp.where(lane_p < HEAD, mtt[0], mtt[1]),
                           jnp.where(lane_p < HEAD, mtt[2], mtt[3]),
                           jnp.where(lane_p < HEAD, mtt[4], zeros_t)], axis=1)
    a_int = jnp.exp(fx + m1 - mtx)
    m1_new = mtx[L - 1:L, :]
    w = jnp.exp(fx[L - 1:L, :] - fx + igx - m1_new)
    a_c = jnp.exp(fx[L - 1:L, :] + m1 - m1_new)
    m1_ref[...] = m1_new

    nums, nqs = [], []
    for p in range(N_PAIRS):
        sl = slice(p * LANES, (p + 1) * LANES)
        cp = cact[:, sl]
        q = _dot(cp, wq_ref[p])
        k = _dot(cp, wk_ref[p])
        sc = _dot_nt(_stack_heads(q), k.astype(BF16))
        p0 = sc[:L] * dws[2 * p]
        p1 = sc[L:] * dws[2 * p + 1] if 2 * p + 1 < N_HEADS else jnp.zeros((L, L), F32)
        a2 = jnp.concatenate([p0, p1], axis=1).astype(BF16)
        intra = _dot(a2, _stack_heads_ones(v[:, sl]))
        c_old = c_ref[p]
        n_old = n_ref[p]
        inter = _dot(q.astype(BF16), jnp.concatenate([c_old, n_old], axis=1).astype(BF16))
        ai = a_int[:, sl]
        nums.append(intra[:, :LANES] + ai * inter[:, :LANES])
        nqs.append(intra[:, LANES:] + ai * inter[:, LANES:])
        kw = k * w[:, sl]
        kw_hi, kw_lo = _split(kw.T, 2)
        upd = _dot(kw_hi, jnp.concatenate([vb[:, sl], ones_l], axis=1))
        nsum = upd[:, LANES:] + _dot(kw_lo, ones_l)
        acp = a_c[:, sl]
        c_ref[p] = acp * c_old + jnp.where(same, upd[:, :LANES], 0.0)
        n_ref[p] = acp * n_old + jnp.where(same, nsum, 0.0)
    num = jnp.concatenate(nums, axis=1)
    nq = jnp.concatenate(nqs, axis=1)
    hh = num / jnp.maximum(jnp.abs(nq), jnp.exp(-mtx))
    o_ref[...] = _head_rms(hh, g_ref[...]) * _sigmoid(mo_ref[...])

    @pl.when(ci == pl.num_programs(1) - 1)
    def _():
        c_out_ref[...] = c_ref[...]
        n_out_ref[...] = n_ref[...]
        m_out_ref[...] = jnp.broadcast_to(m1_ref[...], (8, SEG))


def mlstm_prompt(proj, cw, cb, wq, wk, gb, g, batch, seq):
    L = CHUNK
    nc = seq // L
    x_ig, x_f, x_ft, sel_d = _gate_selectors()
    row = lambda b, c: (b * nc + c, 0)
    const2 = lambda b, c: (0, 0)
    const3 = lambda b, c: (0, 0, 0)
    st_spec = pl.BlockSpec((None, N_PAIRS, LANES, LANES), lambda b, c: (b, 0, 0, 0))
    st_shape = jax.ShapeDtypeStruct((batch, N_PAIRS, LANES, LANES), F32)
    return pl.pallas_call(
        _mlstm_kernel,
        grid=(batch, nc),
        in_specs=[pl.BlockSpec((L, SEG), lambda b, c: (b * nc + c, OFF_MU // SEG)),
                  pl.BlockSpec((L, SEG), lambda b, c: (b * nc + c, OFF_MV // SEG)),
                  pl.BlockSpec((L, SEG), lambda b, c: (b * nc + c, OFF_MO // SEG)),
                  pl.BlockSpec((L, LANES), lambda b, c: (b * nc + c, OFF_GATE // LANES)),
                  pl.BlockSpec((8, SEG), const2),
                  pl.BlockSpec((1, SEG), const2),
                  pl.BlockSpec((N_PAIRS, LANES, LANES), const3),
                  pl.BlockSpec((N_PAIRS, LANES, LANES), const3),
                  pl.BlockSpec((1, LANES), const2),
                  pl.BlockSpec((1, SEG), const2),
                  pl.BlockSpec((LANES, SEG), const2),
                  pl.BlockSpec((LANES, SEG), const2),
                  pl.BlockSpec((LANES, N_HEADS * LANES), const2),
                  pl.BlockSpec((8, LANES), const2)],
        out_specs=[pl.BlockSpec((L, SEG), row), st_spec, st_spec,
                   pl.BlockSpec((None, 8, SEG), lambda b, c: (b, 0, 0))],
        out_shape=[jax.ShapeDtypeStruct((batch * seq, SEG), F32), st_shape, st_shape,
                   jax.ShapeDtypeStruct((batch, 8, SEG), F32)],
        scratch_shapes=[pltpu.VMEM((L + 8, SEG), F32),
                        pltpu.VMEM((N_PAIRS, LANES, LANES), F32),
                        pltpu.VMEM((N_PAIRS, LANES, LANES), F32),
                        pltpu.VMEM((1, SEG), F32),
                        pltpu.VMEM((1, N_HEADS * LANES), F32)],
        compiler_params=_cparams("parallel", "arbitrary"),
    )(proj, proj, proj, proj, cw, cb, wq, wk, gb, g, x_ig, x_f, x_ft, sel_d)


def _swa_kernel(q_ref, kp_ref, kc_ref, vp_ref, vc_ref, o_ref, lse_ref):
    blk = pl.program_id(2)
    L = q_ref.shape[0]
    q = q_ref[...] * (HEAD ** -0.5)
    kcat = jnp.concatenate([kp_ref[...], kc_ref[...]], axis=0).astype(BF16)
    vcat = jnp.concatenate([vp_ref[...], vc_ref[...]], axis=0)
    s = _dot_nt(_stack_heads(q), kcat)
    qi = _iota((2 * L, 2 * L), 0) % L
    kj = _iota((2 * L, 2 * L), 1)
    dist = qi + L - kj
    valid = (dist >= 0) & (dist <= SWA_J) & (blk * L + kj - L >= 0)
    s = jnp.where(valid, s, -jnp.inf)
    mx = jnp.max(s, axis=-1, keepdims=True)
    p = jnp.exp(s - mx)
    p2 = jnp.concatenate([p[:L], p[L:]], axis=1).astype(BF16)
    r = _dot(p2, _stack_heads_ones(vcat))
    den = r[:, LANES:]
    lane = _iota((L, LANES), 1)
    mxt = jnp.where(lane < HEAD, mx[:L], mx[L:])
    o_ref[...] = r[:, :LANES] / den
    lse_ref[...] = mxt + jnp.log(den)


def swa_prompt(q, k, v):
    batch, dil, ls, _ = q.shape
    L = SWA_J
    nb = ls // L
    cur = pl.BlockSpec((None, None, L, LANES), lambda b, r, i: (b, r, i, 0))
    prev = pl.BlockSpec((None, None, L, LANES), lambda b, r, i: (b, r, jnp.maximum(i - 1, 0), 0))
    shape = jax.ShapeDtypeStruct(q.shape, F32)
    return pl.pallas_call(
        _swa_kernel,
        grid=(batch, dil, nb),
        in_specs=[cur, prev, cur, prev, cur],
        out_specs=[cur, cur],
        out_shape=[shape, shape],
        compiler_params=_cparams("parallel", "parallel", "arbitrary"),
    )(q, k, k, v, v)


def _mix_out_kernel(oh_ref, om_ref, o0_ref, o1_ref, o2_ref, l0_ref, l1_ref, l2_ref, x_ref, w_ref, y_ref):
    l0, l1, l2 = l0_ref[...], l1_ref[...], l2_ref[...]
    mx = jnp.maximum(jnp.maximum(l0, l1), l2)
    e0, e1, e2 = jnp.exp(l0 - mx), jnp.exp(l1 - mx), jnp.exp(l2 - mx)
    inv = 1.0 / (e0 + e1 + e2)
    mix = jnp.concatenate([oh_ref[...], om_ref[...], o0_ref[...] * (e0 * inv), o1_ref[...] * (e1 * inv),
                           o2_ref[...] * (e2 * inv)], axis=1)
    y_ref[...] = x_ref[...] + _dot(mix.astype(BF16), w_ref[...])


def mix_out(oh, om, os_, ls_, x, w, tm):
    m = x.shape[0]
    row = lambda i: (i, 0)
    seg = pl.BlockSpec((tm, SEG), row)
    tile = pl.BlockSpec((tm, LANES), row)
    return pl.pallas_call(
        _mix_out_kernel,
        grid=(m // tm,),
        in_specs=[seg, seg, tile, tile, tile, tile, tile, tile,
                  pl.BlockSpec((tm, D_MODEL), row),
                  pl.BlockSpec((3 * SEG, D_MODEL), lambda i: (0, 0))],
        out_specs=pl.BlockSpec((tm, D_MODEL), row),
        out_shape=jax.ShapeDtypeStruct((m, D_MODEL), F32),
        compiler_params=_cparams("parallel"),
    )(oh, om, os_[0], os_[1], os_[2], ls_[0], ls_[1], ls_[2], x, w)


def _ffn_kernel(x_ref, g_ref, wg_ref, wu_ref, wd_ref, fg_ref, o_ref, h_ref, acc_ref, *, final):
    j = pl.program_id(1)

    @pl.when(j == 0)
    def _():
        x = x_ref[...]
        h_ref[...] = _rmsnorm(x, g_ref[...]).astype(BF16)
        acc_ref[...] = x

    h = h_ref[...]
    a = _dot(h, wg_ref[...])
    u = _dot(h, wu_ref[...])
    acc_ref[...] += _dot((a * _sigmoid(a) * u).astype(BF16), wd_ref[...])

    @pl.when(j == pl.num_programs(1) - 1)
    def _():
        y = acc_ref[...]
        o_ref[...] = _rmsnorm(y, fg_ref[...]) if final else y


def ffn_dense(x, g, wg, wu, wd, fg, final, tm, tf):
    m = x.shape[0]
    ff = wg.shape[1]
    return pl.pallas_call(
        functools.partial(_ffn_kernel, final=final),
        grid=(m // tm, ff // tf),
        in_specs=[pl.BlockSpec((tm, D_MODEL), lambda i, j: (i, 0)),
                  pl.BlockSpec((1, D_MODEL), lambda i, j: (0, 0)),
                  pl.BlockSpec((D_MODEL, tf), lambda i, j: (0, j)),
                  pl.BlockSpec((D_MODEL, tf), lambda i, j: (0, j)),
                  pl.BlockSpec((tf, D_MODEL), lambda i, j: (j, 0)),
                  pl.BlockSpec((1, D_MODEL), lambda i, j: (0, 0))],
        out_specs=pl.BlockSpec((tm, D_MODEL), lambda i, j: (i, 0)),
        out_shape=jax.ShapeDtypeStruct((m, D_MODEL), F32),
        scratch_shapes=[pltpu.VMEM((tm, D_MODEL), BF16), pltpu.VMEM((tm, D_MODEL), F32)],
        compiler_params=_cparams("parallel", "arbitrary"),
    )(x, g, wg, wu, wd, fg)


def _moe_kernel(x_ref, g_ref, rw_hi_ref, rw_lo_ref, rb_ref, wg_ref, wu_ref, wd_ref, fg_ref, o_ref,
                h_ref, acc_ref, gate_ref, sel_ref, rank_ref, selt_ref, rankt_ref, *, final, sub, chunk):
    e = pl.program_id(1)
    tm = x_ref.shape[0]
    n_sub = tm // sub
    lane = _iota((tm, LANES), 1)

    @pl.when(e == 0)
    def _():
        x = x_ref[...]
        hf = _rmsnorm(x, g_ref[...])
        h_ref[...] = hf.astype(BF16)
        acc_ref[...] = x
        h_hi, h_lo = _split(hf, 2)
        logits = (_dot(h_hi, rw_hi_ref[...]) + _dot(h_lo, rw_hi_ref[...]) + _dot(h_hi, rw_lo_ref[...])
                  + rb_ref[...])
        lanef = lane.astype(F32)
        lg = jnp.where(lane < N_EXPERTS, logits, -jnp.inf)
        m1 = jnp.max(lg, axis=-1, keepdims=True)
        i1 = jnp.min(jnp.where(lg == m1, lanef, float(LANES)), axis=-1, keepdims=True)
        lg2 = jnp.where(lanef == i1, -jnp.inf, lg)
        m2 = jnp.max(lg2, axis=-1, keepdims=True)
        i2 = jnp.min(jnp.where(lg2 == m2, lanef, float(LANES)), axis=-1, keepdims=True)
        e2 = jnp.exp(m2 - m1)
        w1 = 1.0 / (1.0 + e2)
        gate_ref[...] = jnp.where(lanef == i1, w1, 0.0) + jnp.where(lanef == i2, e2 * w1, 0.0)
        sel = jnp.where(lanef == i1, 1.0, 0.0) + jnp.where(lanef == i2, 1.0, 0.0)
        sel_ref[...] = sel
        earlier = jnp.where(_iota((sub, sub), 0) > _iota((sub, sub), 1), 1.0, 0.0).astype(BF16)
        later = jnp.where(_iota((sub, sub), 0) < _iota((sub, sub), 1), 1.0, 0.0).astype(BF16)
        for s in range(n_sub):
            sel_s = sel[s * sub:(s + 1) * sub]
            rank_ref[s * sub:(s + 1) * sub, :] = _dot(earlier, sel_s.astype(BF16))
            sel_t = sel_s.T
            selt_ref[s] = sel_t
            rankt_ref[s] = _dot(sel_t.astype(BF16), later)

    onlane = lane == e
    ge = jnp.sum(jnp.where(onlane, gate_ref[...], 0.0), axis=-1, keepdims=True)
    sel_c = jnp.sum(jnp.where(onlane, sel_ref[...], 0.0), axis=-1, keepdims=True)
    rank_c = jnp.sum(jnp.where(onlane, rank_ref[...], 0.0), axis=-1, keepdims=True)
    for s in range(n_sub):
        rows = slice(s * sub, (s + 1) * sub)
        sel_r = selt_ref[s, pl.ds(e, 1), :]
        rank_r = rankt_ref[s, pl.ds(e, 1), :]
        count = jnp.sum(sel_r).astype(jnp.int32)
        hs = h_ref[rows, :]
        ge_s, sel_cs, rank_cs = ge[rows], sel_c[rows], rank_c[rows]

        def body(c, carry):
            base = (c * chunk).astype(F32)
            pick = jnp.where((rank_r == _iota((chunk, sub), 0).astype(F32) + base) & (sel_r > 0.0), 1.0, 0.0)
            back = jnp.where((rank_cs == _iota((sub, chunk), 1).astype(F32) + base) & (sel_cs > 0.0), 1.0, 0.0)
            xg = _dot(pick.astype(BF16), hs).astype(BF16)
            a = _dot(xg, wg_ref[...])
            u = _dot(xg, wu_ref[...])
            y = _dot((a * _sigmoid(a) * u).astype(BF16), wd_ref[...]).astype(BF16)
            acc_ref[rows, :] += ge_s * _dot(back.astype(BF16), y)
            return carry

        lax.fori_loop(0, (count + chunk - 1) // chunk, body, 0)

    @pl.when(e == pl.num_programs(1) - 1)
    def _():
        y = acc_ref[...]
        o_ref[...] = _rmsnorm(y, fg_ref[...]) if final else y


def ffn_moe(x, g, rw_hi, rw_lo, rb, wg, wu, wd, fg, final, tm):
    m = x.shape[0]
    ff = wg.shape[2]
    sub = min(tm, MOE_SUB)
    chunk = min(sub, MOE_CHUNK)
    c2 = lambda i, e: (0, 0)
    return pl.pallas_call(
        functools.partial(_moe_kernel, final=final, sub=sub, chunk=chunk),
        grid=(m // tm, N_EXPERTS),
        in_specs=[pl.BlockSpec((tm, D_MODEL), lambda i, e: (i, 0)),
                  pl.BlockSpec((1, D_MODEL), c2),
                  pl.BlockSpec((D_MODEL, LANES), c2),
                  pl.BlockSpec((D_MODEL, LANES), c2),
                  pl.BlockSpec((1, LANES), c2),
                  pl.BlockSpec((None, D_MODEL, ff), lambda i, e: (e, 0, 0)),
                  pl.BlockSpec((None, D_MODEL, ff), lambda i, e: (e, 0, 0)),
                  pl.BlockSpec((None, ff, D_MODEL), lambda i, e: (e, 0, 0)),
                  pl.BlockSpec((1, D_MODEL), c2)],
        out_specs=pl.BlockSpec((tm, D_MODEL), lambda i, e: (i, 0)),
        out_shape=jax.ShapeDtypeStruct((m, D_MODEL), F32),
        scratch_shapes=[pltpu.VMEM((tm, D_MODEL), BF16), pltpu.VMEM((tm, D_MODEL), F32),
                        pltpu.VMEM((tm, LANES), F32), pltpu.VMEM((tm, LANES), F32), pltpu.VMEM((tm, LANES), F32),
                        pltpu.VMEM((tm // sub, LANES, sub), F32), pltpu.VMEM((tm // sub, LANES, sub), F32)],
        compiler_params=_cparams("parallel", "arbitrary"),
    )(x, g, rw_hi, rw_lo, rb, wg, wu, wd, fg)


def _hgrn_step_kernel(p_ref, lb_ref, g_ref, s_ref, o_ref, s_out_ref, qt_ref, kt_ref, dt_ref, vt_ref, ot_ref):
    h = pl.program_id(0)

    @pl.when(h == 0)
    def _():
        logf, kk, q = _hgrn_gates(p_ref[:, OFF_HQ:OFF_HQ + SEG], p_ref[:, OFF_HF:OFF_HF + SEG], lb_ref[...])
        qt_ref[...] = q.T
        kt_ref[...] = kk.T
        dt_ref[...] = jnp.exp(logf).T
        vt_ref[...] = p_ref[:, OFF_HI:OFF_HI + SEG].T
        ot_ref[...] = jnp.zeros_like(ot_ref)

    base = pl.multiple_of(h * HEAD, HEAD)
    vh = vt_ref[pl.ds(base, HEAD), :]

    def body(k, o):
        s_new = dt_ref[pl.ds(base + k, 1), :] * s_ref[k] + kt_ref[pl.ds(base + k, 1), :] * vh
        s_out_ref[k] = s_new
        return o + qt_ref[pl.ds(base + k, 1), :] * s_new

    ot_ref[pl.ds(base, HEAD), :] = lax.fori_loop(0, HEAD, body, jnp.zeros(vh.shape, F32), unroll=8)

    @pl.when(h == N_HEADS - 1)
    def _():
        hg = p_ref[:, OFF_HG:OFF_HG + SEG]
        o_ref[...] = _head_rms(ot_ref[...].T, g_ref[...]) * (hg * _sigmoid(hg))


def hgrn_step(proj, lb, g, state, li):
    n = proj.shape[0]
    c2 = lambda h: (0, 0)
    st = pl.BlockSpec((None, None, HEAD, HEAD, n), lambda h: (li, h, 0, 0, 0))
    return pl.pallas_call(
        _hgrn_step_kernel,
        grid=(N_HEADS,),
        in_specs=[pl.BlockSpec((n, 4 * SEG), c2), pl.BlockSpec((1, SEG), c2), pl.BlockSpec((1, SEG), c2), st],
        out_specs=[pl.BlockSpec((n, SEG), c2),
                   pl.BlockSpec((None, HEAD, HEAD, n), lambda h: (h, 0, 0, 0))],
        out_shape=[jax.ShapeDtypeStruct((n, SEG), F32), jax.ShapeDtypeStruct((N_HEADS, HEAD, HEAD, n), F32)],
        scratch_shapes=[pltpu.VMEM((SEG, n), F32)] * 5,
        compiler_params=_cparams("arbitrary"),
    )(proj, lb, g, state)


def _mlstm_step_kernel(mu_ref, mv_ref, mo_ref, gate_ref, conv_ref, cw_ref, cb_ref, wq_ref, wk_ref, gb_ref,
                       g_ref, c_ref, n_ref, m_ref,
                       o_ref, c_out_ref, n_out_ref, m_out_ref, conv_out_ref,
                       qt_ref, kt_ref, vt_ref, gt_ref, ht_ref, kw_ref):
    h = pl.program_id(0)
    n5 = N_HEADS * HEAD

    @pl.when(h == 0)
    def _():
        mut = mu_ref[...].T[:n5]
        y = cb_ref[...] + cw_ref[CONV_W - 1] * mut
        for j in range(CONV_W - 1):
            y = y + cw_ref[j] * conv_ref[j]
            if j > 0:
                conv_out_ref[j - 1] = conv_ref[j]
        conv_out_ref[CONV_W - 2] = mut
        cact = jnp.concatenate([y * _sigmoid(y), jnp.zeros((SEG - n5, y.shape[1]), F32)], axis=0).astype(BF16)
        for p in range(N_PAIRS):
            sl = slice(p * LANES, (p + 1) * LANES)
            qt_ref[sl, :] = _dot(wq_ref[p], cact[sl])
            kt_ref[sl, :] = _dot(wk_ref[p], cact[sl])
        vt_ref[...] = mv_ref[...].T
        gt_ref[...] = gate_ref[...].T + gb_ref[...]
        ht_ref[...] = jnp.zeros_like(ht_ref)
        m_out_ref[...] = jnp.zeros_like(m_out_ref)

    ig = gt_ref[pl.ds(h, 1), :]
    log_int = _log_sigmoid(gt_ref[pl.ds(h + N_HEADS, 1), :]) + m_ref[pl.ds(h, 1), :]
    mt = jnp.maximum(log_int, ig)
    dw = jnp.exp(ig - mt)
    ai = jnp.exp(log_int - mt)
    m_out_ref[pl.ds(h, 1), :] = mt

    base = pl.multiple_of(h * HEAD, HEAD)
    qh = qt_ref[pl.ds(base, HEAD), :]
    vh = vt_ref[pl.ds(base, HEAD), :]
    kw = kt_ref[pl.ds(base, HEAD), :] * dw
    kw_ref[...] = kw
    n_new = ai * n_ref[...] + kw
    n_out_ref[...] = n_new
    nq = jnp.sum(qh * n_new, axis=0, keepdims=True)

    def body(k, num):
        c_new = ai * c_ref[k] + kw_ref[pl.ds(k, 1), :] * vh
        c_out_ref[k] = c_new
        return num + qt_ref[pl.ds(base + k, 1), :] * c_new

    num = lax.fori_loop(0, HEAD, body, jnp.zeros(vh.shape, F32), unroll=8)
    hh = num / jnp.maximum(jnp.abs(nq), jnp.exp(-mt))
    ht_ref[pl.ds(base, HEAD), :] = hh * lax.rsqrt(jnp.mean(hh * hh, axis=0, keepdims=True) + EPS)

    @pl.when(h == N_HEADS - 1)
    def _():
        o_ref[...] = ht_ref[...].T * g_ref[...] * _sigmoid(mo_ref[...])


def mlstm_step(proj, conv, cw, cb, wq_t, wk_t, gb, g, c_state, n_state, m_state, li):
    n = proj.shape[0]
    n5 = N_HEADS * HEAD
    c2 = lambda h: (0, 0)
    c3 = lambda h: (0, 0, 0)
    return pl.pallas_call(
        _mlstm_step_kernel,
        grid=(N_HEADS,),
        in_specs=[pl.BlockSpec((n, SEG), lambda h: (0, OFF_MU // SEG)),
                  pl.BlockSpec((n, SEG), lambda h: (0, OFF_MV // SEG)),
                  pl.BlockSpec((n, SEG), lambda h: (0, OFF_MO // SEG)),
                  pl.BlockSpec((n, LANES), lambda h: (0, OFF_GATE // LANES)),
                  pl.BlockSpec((None, CONV_W - 1, n5, n), lambda h: (li, 0, 0, 0)),
                  pl.BlockSpec((CONV_W, n5, n), c3), pl.BlockSpec((n5, n), c2),
                  pl.BlockSpec((N_PAIRS, LANES, LANES), c3), pl.BlockSpec((N_PAIRS, LANES, LANES), c3),
                  pl.BlockSpec((LANES, n), c2), pl.BlockSpec((1, SEG), c2),
                  pl.BlockSpec((None, None, HEAD, HEAD, n), lambda h: (li, h, 0, 0, 0)),
                  pl.BlockSpec((None, None, HEAD, n), lambda h: (li, h, 0, 0)),
                  pl.BlockSpec((8, n), c2)],
        out_specs=[pl.BlockSpec((n, SEG), c2),
                   pl.BlockSpec((None, HEAD, HEAD, n), lambda h: (h, 0, 0, 0)),
                   pl.BlockSpec((None, HEAD, n), lambda h: (h, 0, 0)),
                   pl.BlockSpec((8, n), c2),
                   pl.BlockSpec((CONV_W - 1, n5, n), c3)],
        out_shape=[jax.ShapeDtypeStruct((n, SEG), F32),
                   jax.ShapeDtypeStruct((N_HEADS, HEAD, HEAD, n), F32),
                   jax.ShapeDtypeStruct((N_HEADS, HEAD, n), F32),
                   jax.ShapeDtypeStruct((8, n), F32),
                   jax.ShapeDtypeStruct((CONV_W - 1, n5, n), F32)],
        scratch_shapes=[pltpu.VMEM((SEG, n), F32), pltpu.VMEM((SEG, n), F32), pltpu.VMEM((SEG, n), F32),
                        pltpu.VMEM((LANES, n), F32), pltpu.VMEM((SEG, n), F32), pltpu.VMEM((HEAD, n), F32)],
        compiler_params=_cparams("arbitrary"),
    )(proj, proj, proj, proj, conv, cw, cb, wq_t, wk_t, gb, g, c_state, n_state, m_state)


def _lane_tile(x, reps):
    return x if reps == 1 else jnp.concatenate([x] * reps, axis=1)


def _swa_step_kernel(*refs, nb, aliased):
    q_ref, k_ref, v_ref, c0_ref, c1_ref, c2_ref = refs[:6]
    o_ref, n0_ref, n1_ref, n2_ref, bt_ref, ot_ref = refs[-6:]
    i = pl.program_id(0)
    n = q_ref.shape[0]

    @pl.when(i == 0)
    def _():
        bt_ref[0:SEG, :] = (q_ref[...] * (HEAD ** -0.5)).T
        bt_ref[SEG:2 * SEG, :] = k_ref[...].T
        bt_ref[2 * SEG:3 * SEG, :] = v_ref[...].T
        ot_ref[...] = jnp.zeros_like(ot_ref)

    lane_n = _iota((HEAD, n), 1)
    for t in range(nb):
        tok = i * nb + t
        onehot = jnp.where(_iota((n, LANES), 0) == tok, 1.0, 0.0).astype(BF16)
        bc = _dot_sel(bt_ref[...], onehot)
        outs, lses = [], []
        for g, (c_ref, n_ref) in enumerate(((c0_ref, n0_ref), (c1_ref, n1_ref), (c2_ref, n2_ref))):
            w = c_ref.shape[-1]
            dil = SWA_PAIRS[g][1]
            reps = w // LANES
            live = _iota((1, w), 1) % dil == 0
            last = _iota((HEAD, w), 1) == w - 1
            for j in range(2):
                r = g * LANES + j * HEAD
                qb = bc[r:r + HEAD]
                kb = bc[SEG + r:SEG + r + HEAD]
                vb = bc[2 * SEG + r:2 * SEG + r + HEAD]
                kc = c_ref[t, 0, j]
                vc = c_ref[t, 1, j]
                s = jnp.where(live, jnp.sum(_lane_tile(qb, reps) * kc, axis=0, keepdims=True), -jnp.inf)
                sn = jnp.sum(qb * kb, axis=0, keepdims=True)
                mx = jnp.maximum(jnp.max(s, axis=1, keepdims=True), sn)
                p = jnp.exp(s - _lane_tile(mx, reps))
                pn = jnp.exp(sn - mx)
                den = jnp.sum(p, axis=1, keepdims=True) + pn
                outs.append((jnp.sum(p * vc, axis=1, keepdims=True) + pn * vb) / den)
                lses.append(mx + jnp.log(den))
                n_ref[t, 0, j] = jnp.where(last, _lane_tile(kb, reps), pltpu.roll(kc, w - 1, axis=1))
                n_ref[t, 1, j] = jnp.where(last, _lane_tile(vb, reps), pltpu.roll(vc, w - 1, axis=1))
        for j in range(2):
            ls = [lses[2 * g + j] for g in range(N_GROUPS)]
            mx = jnp.maximum(jnp.maximum(ls[0], ls[1]), ls[2])
            es = [jnp.exp(l - mx) for l in ls]
            inv = 1.0 / (es[0] + es[1] + es[2])
            for g in range(N_GROUPS):
                r = g * LANES + j * HEAD
                col = outs[2 * g + j] * (es[g] * inv)
                col = col if n == LANES else col[:, :n]
                ot_ref[r:r + HEAD, :] = jnp.where(lane_n == tok, col, ot_ref[r:r + HEAD, :])

    @pl.when(i == pl.num_programs(0) - 1)
    def _():
        o_ref[...] = ot_ref[...].T


def swa_step(proj, caches, prev, li, nb):
    n = proj.shape[0]
    aliased = prev is not None
    c2 = lambda i: (0, 0)

    def cspec(c):
        return pl.BlockSpec((None, nb, 2, 2, HEAD, c.shape[-1]), lambda i: (li, i, 0, 0, 0, 0))

    in_specs = [pl.BlockSpec((n, SEG), lambda i: (0, OFF_SQ // SEG)),
                pl.BlockSpec((n, SEG), lambda i: (0, OFF_SK // SEG)),
                pl.BlockSpec((n, SEG), lambda i: (0, OFF_SV // SEG))] + [cspec(c) for c in caches]
    args = [proj, proj, proj] + list(caches)
    aliases = {}
    if aliased:
        in_specs += [pl.BlockSpec(memory_space=pl.ANY)] * N_GROUPS
        aliases = {len(args) + g: 1 + g for g in range(N_GROUPS)}
        args += list(prev)
    res = pl.pallas_call(
        functools.partial(_swa_step_kernel, nb=nb, aliased=aliased),
        grid=(n // nb,),
        in_specs=in_specs,
        out_specs=[pl.BlockSpec((n, SEG), c2)] + [cspec(c) for c in caches],
        out_shape=[jax.ShapeDtypeStruct((n, SEG), F32)] + [jax.ShapeDtypeStruct(c.shape, F32) for c in caches],
        scratch_shapes=[pltpu.VMEM((3 * SEG, n), F32), pltpu.VMEM((SEG, n), F32)],
        input_output_aliases=aliases,
        compiler_params=_cparams("arbitrary"),
    )(*args)
    return res[0], res[1:]


def _mix_out_step_kernel(oh_ref, om_ref, os_ref, x_ref, w_ref, y_ref):
    mix = jnp.concatenate([oh_ref[...], om_ref[...], os_ref[...]], axis=1)
    y_ref[...] = x_ref[...] + _dot(mix.astype(BF16), w_ref[...])


def mix_out_step(oh, om, os_, x, w):
    m = x.shape[0]
    c2 = lambda i: (0, 0)
    seg = pl.BlockSpec((m, SEG), c2)
    return pl.pallas_call(
        _mix_out_step_kernel,
        grid=(1,),
        in_specs=[seg, seg, seg, pl.BlockSpec((m, D_MODEL), c2), pl.BlockSpec((3 * SEG, D_MODEL), c2)],
        out_specs=pl.BlockSpec((m, D_MODEL), c2),
        out_shape=jax.ShapeDtypeStruct((m, D_MODEL), F32),
        compiler_params=_cparams("arbitrary"),
    )(oh, om, os_, x, w)


def _pad_cols(a, n):
    return jnp.pad(a, [(0, 0)] * (a.ndim - 1) + [(0, n - a.shape[-1])])


def _pad_rows(a, n):
    return jnp.pad(a, [(0, n - a.shape[0])] + [(0, 0)] * (a.ndim - 1))


def _relayout_w_in(wt):
    segs = jnp.split(wt, np.cumsum(IN_SPLITS)[:-1].tolist(), axis=0)
    hq, hf, hi, hg, mu, mv, mo, mig, mfg, sq, sk, sv = segs
    rows = [_pad_rows(s, SEG) for s in (hq, hf, hi, hg, mu, mv, mo)] + [sq, sk, sv]
    rows.append(_pad_rows(jnp.concatenate([mig, mfg], axis=0), LANES))
    return jnp.concatenate(rows, axis=0).astype(BF16)


def _relayout_w_out(w):
    n5 = N_HEADS * HEAD
    rows = [w[:n5], jnp.zeros((SEG - n5, D_MODEL), w.dtype), w[n5:2 * n5], jnp.zeros((SEG - n5, D_MODEL), w.dtype),
            w[2 * n5:]]
    return jnp.concatenate(rows, axis=0).astype(BF16)


def _pair_block_diag(w):
    w6 = jnp.concatenate([w, jnp.zeros((1, HEAD, HEAD), w.dtype)], axis=0)
    z = jnp.zeros((HEAD, HEAD), w.dtype)
    return jnp.stack([jnp.block([[w6[2 * p], z], [z, w6[2 * p + 1]]]) for p in range(N_PAIRS)], axis=0)


def _row(a, n):
    return _pad_cols(a.reshape(1, -1).astype(F32), n)


def _unpair_state(st):
    heads = []
    for h in range(N_HEADS):
        p, j = divmod(h, 2)
        heads.append(st[:, p, j * HEAD:(j + 1) * HEAD, j * HEAD:(j + 1) * HEAD])
    return jnp.stack(heads, axis=1)


def kernel(x_prompt, x_sample, state_hgrn, state_mlstm_C, state_mlstm_n, state_mlstm_m, state_mlstm_conv,
           cache_swa_w128, cache_swa_w512, cache_swa_w2048, norm1_g, norm2_g, final_norm_g, w_in, w_out,
           hgrn_lb_param, hgrn_onorm_g, mlstm_conv_w, mlstm_conv_b, mlstm_wq, mlstm_wk, mlstm_ig_b,
           mlstm_fg_b, mlstm_onorm_g, ffn_w_gate, ffn_w_up, ffn_w_down, moe_router_w, moe_router_b,
           moe_w_gate, moe_w_up, moe_w_down):
    batch, seq, _ = x_prompt.shape
    n_dec = x_sample.shape[0]
    depth = w_in.shape[0]
    caches = (cache_swa_w128, cache_swa_w512, cache_swa_w2048)

    p_lb = jax.nn.softmax(hgrn_lb_param.astype(F32), axis=0)
    cs = jnp.cumsum(p_lb, axis=0)
    lb_all = cs - cs[0:1]

    xp = x_prompt.reshape(batch * seq, D_MODEL)
    xs = x_sample.reshape(n_dec, D_MODEL)
    out_p = {k: [] for k in ("hgrn", "C", "n", "m", "conv", "swa0", "swa1", "swa2")}
    out_s = {k: [] for k in ("hgrn", "C", "n", "m", "conv")}

    w_in_t = w_in.transpose(2, 0, 1)

    n5 = N_HEADS * HEAD
    hgrn_t = state_hgrn.astype(F32).transpose(0, 2, 3, 4, 1)
    c_t = state_mlstm_C.astype(F32).transpose(0, 2, 3, 4, 1)
    n_t = state_mlstm_n.astype(F32).transpose(0, 2, 3, 1)
    conv_t = state_mlstm_conv.astype(F32).transpose(0, 2, 3, 1)
    caches_t = [c.astype(F32).transpose(0, 1, 3, 4, 5, 2) for c in caches]
    new_caches = None

    for li in range(depth):
        w_in_l = _relayout_w_in(w_in_t[:, li, :])
        w_out_l = _relayout_w_out(w_out[li])
        n1 = norm1_g[li].reshape(1, D_MODEL)
        n2 = norm2_g[li].reshape(1, D_MODEL)
        lb = jnp.pad(lb_all[li].reshape(1, -1), ((0, 0), (0, SEG - N_HEADS * HEAD)), constant_values=0.5)
        hg_g = _row(hgrn_onorm_g[li], SEG)
        ml_g = _row(mlstm_onorm_g[li], SEG)
        cw = jnp.pad(mlstm_conv_w[li].astype(F32), ((0, 8 - CONV_W), (0, SEG - N_HEADS * HEAD)))
        cb = _row(mlstm_conv_b[li], SEG)
        wq = _pair_block_diag(mlstm_wq[li]).astype(BF16)
        wk = (_pair_block_diag(mlstm_wk[li]) * (HEAD ** -0.5)).astype(BF16)
        gb = _row(jnp.concatenate([mlstm_ig_b[li], mlstm_fg_b[li]]), LANES)

        proj = norm_proj(xp, n1, w_in_l, 256)
        oh, s_h = hgrn_prompt(proj, lb, hg_g, batch, seq)
        om, s_c, s_n, s_m = mlstm_prompt(proj, cw, cb, wq, wk, gb, ml_g, batch, seq)
        p3 = proj.reshape(batch, seq, N_PROJ)
        os_, ls_ = [], []
        for g, (win, dil) in enumerate(SWA_PAIRS):
            def strided(off):
                a = p3[:, :, off + g * LANES:off + (g + 1) * LANES]
                return a.reshape(batch, seq // dil, dil, LANES).transpose(0, 2, 1, 3)
            o, lse = swa_prompt(strided(OFF_SQ), strided(OFF_SK), strided(OFF_SV))
            os_.append(o.transpose(0, 2, 1, 3).reshape(batch * seq, LANES))
            ls_.append(lse.transpose(0, 2, 1, 3).reshape(batch * seq, LANES))
            keep = min(win, seq)
            kv = jnp.stack([p3[:, seq - keep:, OFF_SK + g * LANES:OFF_SK + (g + 1) * LANES],
                            p3[:, seq - keep:, OFF_SV + g * LANES:OFF_SV + (g + 1) * LANES]], axis=2)
            out_p["swa%d" % g].append(kv.reshape(batch, keep, 2, 2, HEAD))
        xp = mix_out(oh, om, os_, ls_, xp, w_out_l, 512)
        out_p["hgrn"].append(_unpair_state(s_h))
        out_p["C"].append(_unpair_state(s_c))
        out_p["n"].append(jnp.stack([s_n[:, h // 2, (h % 2) * HEAD:(h % 2 + 1) * HEAD, (h % 2) * HEAD]
                                     for h in range(N_HEADS)], axis=1))
        out_p["m"].append(s_m[:, 0, 0:N_HEADS * HEAD:HEAD])
        out_p["conv"].append(p3[:, seq - (CONV_W - 1):, OFF_MU:OFF_MU + N_HEADS * HEAD])

        projs = norm_proj(xs, n1, w_in_l, n_dec)
        ohs, hs_new = hgrn_step(projs, lb, hg_g, hgrn_t, li)
        cw_t = jnp.broadcast_to(mlstm_conv_w[li].astype(F32)[:, :, None], (CONV_W, n5, n_dec))
        cb_t = jnp.broadcast_to(mlstm_conv_b[li].astype(F32)[:, None], (n5, n_dec))
        gb_t = jnp.broadcast_to(_pad_cols(gb, LANES).reshape(LANES, 1), (LANES, n_dec))
        m_in = jnp.pad(state_mlstm_m[li].astype(F32).T, ((0, 8 - N_HEADS), (0, 0)))
        oms, c_new, n_new, m_new, conv_new = mlstm_step(
            projs, conv_t, cw_t, cb_t, wq.transpose(0, 2, 1), wk.transpose(0, 2, 1), gb_t, ml_g,
            c_t, n_t, m_in, li)
        oss, new_caches = swa_step(projs, caches_t, new_caches, li, 2)
        xs = mix_out_step(ohs, oms, oss, xs, w_out_l)
        out_s["hgrn"].append(hs_new)
        out_s["C"].append(c_new)
        out_s["n"].append(n_new)
        out_s["m"].append(m_new[:N_HEADS])
        out_s["conv"].append(conv_new)

        final = li == depth - 1
        fg = final_norm_g.reshape(1, D_MODEL)
        j = li // 2
        if li % 2 == 0:
            wg, wu, wd = (ffn_w_gate[j].astype(BF16), ffn_w_up[j].astype(BF16), ffn_w_down[j].astype(BF16))
            xp = ffn_dense(xp, n2, wg, wu, wd, fg, final, 512, 1408)
            xs = ffn_dense(xs, n2, wg, wu, wd, fg, final, n_dec, 1408)
        else:
            rw = _pad_cols(moe_router_w[j].astype(F32), LANES)
            rw_hi = rw.astype(BF16)
            rw_lo = (rw - rw_hi.astype(F32)).astype(BF16)
            rb = _row(moe_router_b[j], LANES)
            wg, wu, wd = (moe_w_gate[j].astype(BF16), moe_w_up[j].astype(BF16), moe_w_down[j].astype(BF16))
            xp = ffn_moe(xp, n2, rw_hi, rw_lo, rb, wg, wu, wd, fg, final, 1024)
            xs = ffn_moe(xs, n2, rw_hi, rw_lo, rb, wg, wu, wd, fg, final, n_dec)

    stk = lambda lst: jnp.stack(lst, axis=0)
    cache_out = [c.transpose(0, 1, 5, 2, 3, 4) for c in new_caches]
    return (xp.reshape(batch, seq, D_MODEL), xs.reshape(n_dec, 1, D_MODEL),
            stk(out_p["hgrn"]), stk(out_s["hgrn"]).transpose(0, 4, 1, 2, 3),
            stk(out_p["C"]), stk(out_s["C"]).transpose(0, 4, 1, 2, 3),
            stk(out_p["n"]), stk(out_s["n"]).transpose(0, 3, 1, 2),
            stk(out_p["m"]), stk(out_s["m"]).transpose(0, 2, 1),
            stk(out_p["conv"]), stk(out_s["conv"]).transpose(0, 3, 1, 2),
            stk(out_p["swa0"]), cache_out[0],
            stk(out_p["swa1"]), cache_out[1],
            stk(out_p["swa2"]), cache_out[2])
```
